```python
import math
import jax, jax.numpy as jnp
from jax import lax
import numpy as np

D_MODEL = 1024
BATCH = 4
SEQ = 4096
DEPTH = 4
DEC_BATCH = 32
DEC_SEQ = 16
PAST_LEN = 4096

CHUNK = 64
N_EVEN = (DEPTH + 1) // 2
N_ODD = DEPTH // 2
N_VRES = max(N_ODD - 1, 0)
MIX_WIDTH = D_MODEL
POOL_WIDTH = MIX_WIDTH // 2
POOL_GROUPS = 4
POOL_GW = POOL_WIDTH // POOL_GROUPS
POOL_WINDOWS = (2, 4, 8, 16)
POOL_HIST = max(POOL_WINDOWS) - 1
DIFF_WIDTH = MIX_WIDTH - POOL_WIDTH
DIFF_DH = 64
DIFF_HEADS = DIFF_WIDTH // (2 * DIFF_DH)
DIFF_VD = 2 * DIFF_DH
EVEN_IN = POOL_WIDTH + 3 * DIFF_WIDTH
ATTN_Q_BLOCK = 128
RW_N = 64
RW_HEADS = D_MODEL // RW_N
RW_DECAY_LORA = 64
RW_A_LORA = 64
RW_V_LORA = 32
RW_G_LORA = 160
RW_LN_EPS = 64e-5
N_MEM = 256
XA_HEADS = 4
XA_DH = D_MODEL // XA_HEADS
D_FF = -(-8 * D_MODEL // (3 * 256)) * 256
NORM_EPS = 1e-6
NEG_INF = -1e30

kernel_name = 'hybrid_pool_diffattn_rwkv7_stream_step'


def rmsnorm(x, g):
    x32 = x.astype(jnp.float32)
    y = x32 * lax.rsqrt(jnp.mean(x32 * x32, axis=-1, keepdims=True) + NORM_EPS)
    return (y * g.astype(jnp.float32)).astype(x.dtype)


def swiglu(h, wg, wu, wd):
    return (jax.nn.silu(h @ wg) * (h @ wu)) @ wd


def pool_mixer(u, hist, pos, w_grp, scale):
    T = u.shape[1]
    full = jnp.concatenate([hist.astype(u.dtype), u], axis=1)
    csum = jnp.cumsum(full.astype(jnp.float32), axis=1)
    csum = jnp.concatenate([jnp.zeros_like(csum[:, :1]), csum], axis=1)
    end = csum[:, POOL_HIST + 1:POOL_HIST + 1 + T]
    outs = []
    for g, w in enumerate(POOL_WINDOWS):
        sl = slice(g * POOL_GW, (g + 1) * POOL_GW)
        start = csum[:, POOL_HIST + 1 - w:POOL_HIST + 1 - w + T, sl]
        cnt = jnp.minimum(pos + 1, w).astype(jnp.float32)[None, :, None]
        pooled = ((end[..., sl] - start) / cnt - u[..., sl].astype(jnp.float32)).astype(u.dtype)
        outs.append(pooled @ w_grp[g])
    y = jnp.concatenate(outs, axis=-1) * scale
    return y, full[:, -POOL_HIST:]


def diff_attend(q, k, v, q_pos, k_pos, lam):
    s = jnp.einsum('bqhmd,bkhmd->bhmqk', q, k).astype(jnp.float32) * (DIFF_DH ** -0.5)
    mask = (k_pos[None, :] // CHUNK) <= (q_pos[:, None] // CHUNK)
    p = jax.nn.softmax(jnp.where(mask, s, NEG_INF), axis=-1)
    p = p[:, :, 0] - lam * p[:, :, 1]
    return jnp.einsum('bhqk,bkhe->bqhe', p.astype(v.dtype), v)


def diff_attention(q, k, v, q_pos, k_pos, lam):
    B, Tq = q.shape[:2]
    if Tq <= ATTN_Q_BLOCK:
        return diff_attend(q, k, v, q_pos, k_pos, lam)
    nb = Tq // ATTN_Q_BLOCK
    qb = jnp.moveaxis(q.reshape(B, nb, ATTN_Q_BLOCK, *q.shape[2:]), 1, 0)
    pb = q_pos.reshape(nb, ATTN_Q_BLOCK)
    ob = lax.map(lambda a: diff_attend(a[0], k, v, a[1], k_pos, lam), (qb, pb))
    return jnp.moveaxis(ob, 0, 1).reshape(B, Tq, *ob.shape[3:])


def even_mixer(h, pos, k_pos, k_past, v_past, pool_hist, layer_idx, w_in, pool_w, pool_scale,
               lq1, lk1, lq2, lk2, subln_g, w_out):
    B, T, _ = h.shape
    z = h @ w_in
    o1, o2, o3 = POOL_WIDTH, POOL_WIDTH + DIFF_WIDTH, POOL_WIDTH + 2 * DIFF_WIDTH
    u = z[..., :o1]
    q = z[..., o1:o2].reshape(B, T, DIFF_HEADS, 2, DIFF_DH)
    k_new = z[..., o2:o3].reshape(B, T, DIFF_HEADS, 2 * DIFF_DH)
    v_new = z[..., o3:].reshape(B, T, DIFF_HEADS, DIFF_VD)
    pool_out, pool_state = pool_mixer(u, pool_hist, pos, pool_w, pool_scale)
    if k_past is None:
        k_all, v_all = k_new, v_new
    else:
        k_all = jnp.concatenate([k_past.astype(k_new.dtype), k_new], axis=1)
        v_all = jnp.concatenate([v_past.astype(v_new.dtype), v_new], axis=1)
    lam_init = 0.8 - 0.6 * math.exp(-0.3 * layer_idx)
    f32 = jnp.float32
    lam = (jnp.exp(jnp.sum(lq1.astype(f32) * lk1.astype(f32)))
           - jnp.exp(jnp.sum(lq2.astype(f32) * lk2.astype(f32))) + lam_init)
    a = diff_attention(q, k_all.reshape(B, -1, DIFF_HEADS, 2, DIFF_DH), v_all, pos, k_pos, lam)
    a = rmsnorm(a, subln_g) * (1.0 - lam_init)
    y = jnp.concatenate([pool_out, a.reshape(B, T, DIFF_WIDTH).astype(pool_out.dtype)], axis=-1) @ w_out
    return y, k_new, v_new, pool_state


def rwkv_mixer(h, shift_prev, S0, v_first, vres, mu, wr, wk, wv, wo, w0, w1, w2, a0, a1, a2,
               g1, g2, k_k, k_a, r_k, lnx_g, lnx_b):
    B, T, D = h.shape
    f32 = jnp.float32
    h_prev = jnp.concatenate([shift_prev[:, None].astype(h.dtype), h[:, :-1]], axis=1)
    xx = h_prev - h
    xr, xw, xk, xv, xa, xg = [h + xx * mu[i] for i in range(6)]
    r = xr @ wr
    w = -jax.nn.softplus(-(w0 + jnp.tanh(xw @ w1) @ w2)) - 0.5
    k = xk @ wk
    v = xv @ wv
    if vres is None:
        v_first = v
    else:
        v0, v1, v2 = vres
        v = v + (v_first - v) * jax.nn.sigmoid(v0 + (xv @ v1) @ v2)
    a = jax.nn.sigmoid(a0 + (xa @ a1) @ a2)
    g = jax.nn.sigmoid(xg @ g1) @ g2
    heads = lambda t: t.reshape(B, T, RW_HEADS, RW_N).astype(f32)
    kk = heads(k * k_k)
    kk = kk / jnp.maximum(jnp.sqrt(jnp.sum(kk * kk, axis=-1, keepdims=True)), 1e-12)
    k = k * (1 + (a - 1) * k_a)
    decay = jnp.exp(-jnp.exp(w.astype(f32)))
    rh, kh, vh, ah, dh = heads(r), heads(k), heads(v), heads(a), heads(decay)

    def step(S, inp):
        r_t, d_t, k_t, v_t, kk_t, a_t = inp
        sa = jnp.einsum('bhvk,bhk->bhv', S, -kk_t)
        S = (S * d_t[:, :, None, :] + sa[..., None] * (kk_t * a_t)[:, :, None, :]
             + v_t[..., None] * k_t[:, :, None, :])
        return S, jnp.einsum('bhvk,bhk->bhv', S, r_t)

    xs = tuple(jnp.moveaxis(t, 1, 0) for t in (rh, dh, kh, vh, kk, ah))
    S_fin, y = lax.scan(step, S0.astype(f32), xs)
    y = jnp.moveaxis(y, 0, 1)
    m = jnp.mean(y, axis=-1, keepdims=True)
    var = jnp.mean(jnp.square(y - m), axis=-1, keepdims=True)
    y = (y - m) * lax.rsqrt(var + RW_LN_EPS)
    y = y.reshape(B, T, D) * lnx_g.astype(f32) + lnx_b.astype(f32)
    bonus = jnp.sum(rh * kh * r_k.astype(f32), axis=-1, keepdims=True) * vh
    y = (y + bonus.reshape(B, T, D)).astype(h.dtype)
    return (y * g) @ wo, h[:, -1], S_fin, v_first


def cross_attn(h, mk, mv, wq, wo):
    B, T, _ = h.shape
    q = (h @ wq).reshape(B, T, XA_HEADS, XA_DH)
    s = jnp.einsum('bqhd,bkhd->bhqk', q, mk.astype(q.dtype)).astype(jnp.float32) * (XA_DH ** -0.5)
    p = jax.nn.softmax(s, axis=-1).astype(q.dtype)
    o = jnp.einsum('bhqk,bkhd->bqhd', p, mv.astype(q.dtype)).reshape(B, T, D_MODEL)
    return o @ wo


def trunk(x, pos, k_pos, diff_k_past, diff_v_past, pool_hist, rw_shift, rw_state, mem_k, mem_v, P):
    new_k, new_v, new_pool, new_shift, new_S = [], [], [], [], []
    v_first = None
    for l in range(DEPTH):
        h = rmsnorm(x, P['norm_mix_g'][l])
        if l % 2 == 0:
            e = l // 2
            kp = None if diff_k_past is None else diff_k_past[e]
            vp = None if diff_v_past is None else diff_v_past[e]
            y, kn, vn, ps = even_mixer(h, pos, k_pos, kp, vp, pool_hist[e], l, P['ev_w_in'][e],
                                       P['ev_pool_w'][e], P['ev_pool_scale'][e], P['ev_lam_q1'][e],
                                       P['ev_lam_k1'][e], P['ev_lam_q2'][e], P['ev_lam_k2'][e],
                                       P['ev_subln_g'][e], P['ev_w_out'][e])
            new_k.append(kn)
            new_v.append(vn)
            new_pool.append(ps)
        else:
            o = l // 2
            vres = None if o == 0 else (P['rw_v0'][o - 1], P['rw_v1'][o - 1], P['rw_v2'][o - 1])
            y, sh, S, v_first = rwkv_mixer(h, rw_shift[o], rw_state[o], v_first, vres, P['rw_mu'][o],
                                           P['rw_wr'][o], P['rw_wk'][o], P['rw_wv'][o], P['rw_wo'][o],
                                           P['rw_w0'][o], P['rw_w1'][o], P['rw_w2'][o], P['rw_a0'][o],
                                           P['rw_a1'][o], P['rw_a2'][o], P['rw_g1'][o], P['rw_g2'][o],
                                           P['rw_k_k'][o], P['rw_k_a'][o], P['rw_r_k'][o],
                                           P['rw_lnx_g'][o], P['rw_lnx_b'][o])
            new_shift.append(sh)
            new_S.append(S)
        x = x + y
        h = rmsnorm(x, P['norm_xa_g'][l])
        x = x + cross_attn(h, mem_k[l], mem_v[l], P['xa_wq'][l], P['xa_wo'][l])
        h = rmsnorm(x, P['norm_ffn_g'][l])
        x = x + swiglu(h, P['ffn_wg'][l], P['ffn_wu'][l], P['ffn_wd'][l])
    y = rmsnorm(x, P['final_norm_g'])
    return y, jnp.stack(new_k), jnp.stack(new_v), jnp.stack(new_pool), jnp.stack(new_shift), jnp.stack(new_S)


def setup_inputs(seed: int = 0) -> dict:
    key = jax.random.key(seed)
    ks = iter(jax.random.split(key, 64))
    nrm = lambda shape, s=1.0: jax.random.normal(next(ks), shape, jnp.float32) * s
    gain = lambda shape: 1.0 + 0.02 * jax.random.normal(next(ks), shape, jnp.float32)
    D = D_MODEL
    return {
        'x_prompt': nrm((BATCH, SEQ, D)),
        'x_sample': nrm((DEC_BATCH, DEC_SEQ, D)),
        'cache_diff_k': nrm((N_EVEN, DEC_BATCH, PAST_LEN, DIFF_HEADS, 2 * DIFF_DH)),
        'cache_diff_v': nrm((N_EVEN, DEC_BATCH, PAST_LEN, DIFF_HEADS, DIFF_VD)),
        'state_pool': nrm((N_EVEN, DEC_BATCH, POOL_HIST, POOL_WIDTH)),
        'state_rw_shift': nrm((N_ODD, DEC_BATCH, D)),
        'state_rw_wkv': nrm((N_ODD, DEC_BATCH, RW_HEADS, RW_N, RW_N), 0.5),
        'cache_mem_k': nrm((DEPTH, DEC_BATCH, N_MEM, XA_HEADS, XA_DH)),
        'cache_mem_v': nrm((DEPTH, DEC_BATCH, N_MEM, XA_HEADS, XA_DH)),
        'mem_prompt': nrm((BATCH, N_MEM, D)),
        'norm_mix_g': gain((DEPTH, D)),
        'norm_xa_g': gain((DEPTH, D)),
        'norm_ffn_g': gain((DEPTH, D)),
        'final_norm_g': gain((D,)),
        'ev_w_in': nrm((N_EVEN, D, EVEN_IN), D ** -0.5),
        'ev_pool_w': nrm((N_EVEN, POOL_GROUPS, POOL_GW, POOL_GW), POOL_GW ** -0.5),
        'ev_pool_scale': gain((N_EVEN, POOL_WIDTH)),
        'ev_lam_q1': nrm((N_EVEN, DIFF_DH), 0.1),
        'ev_lam_k1': nrm((N_EVEN, DIFF_DH), 0.1),
        'ev_lam_q2': nrm((N_EVEN, DIFF_DH), 0.1),
        'ev_lam_k2': nrm((N_EVEN, DIFF_DH), 0.1),
        'ev_subln_g': gain((N_EVEN, DIFF_VD)),
        'ev_w_out': nrm((N_EVEN, MIX_WIDTH, D), MIX_WIDTH ** -0.5),
        'rw_mu': jax.random.uniform(next(ks), (N_ODD, 6, D), jnp.float32),
        'rw_wr': nrm((N_ODD, D, D), D ** -0.5),
        'rw_wk': nrm((N_ODD, D, D), D ** -0.5),
        'rw_wv': nrm((N_ODD, D, D), D ** -0.5),
        'rw_wo': nrm((N_ODD, D, D), D ** -0.5),
        'rw_w0': jnp.linspace(-6.5, -1.5, D, dtype=jnp.float32)[None] + nrm((N_ODD, D), 0.1),
        'rw_w1': nrm((N_ODD, D, RW_DECAY_LORA), D ** -0.5),
        'rw_w2': nrm((N_ODD, RW_DECAY_LORA, D), 0.1 * RW_DECAY_LORA ** -0.5),
        'rw_a0': nrm((N_ODD, D), 0.1),
        'rw_a1': nrm((N_ODD, D, RW_A_LORA), D ** -0.5),
        'rw_a2': nrm((N_ODD, RW_A_LORA, D), 0.1 * RW_A_LORA ** -0.5),
        'rw_v0': nrm((N_VRES, D), 0.1),
        'rw_v1': nrm((N_VRES, D, RW_V_LORA), D ** -0.5),
        'rw_v2': nrm((N_VRES, RW_V_LORA, D), 0.1 * RW_V_LORA ** -0.5),
        'rw_g1': nrm((N_ODD, D, RW_G_LORA), D ** -0.5),
        'rw_g2': nrm((N_ODD, RW_G_LORA, D), RW_G_LORA ** -0.5),
        'rw_k_k': 0.85 + nrm((N_ODD, D), 0.02),
        'rw_k_a': gain((N_ODD, D)),
        'rw_r_k': nrm((N_ODD, RW_HEADS, RW_N), 0.1),
        'rw_lnx_g': gain((N_ODD, D)),
        'rw_lnx_b': nrm((N_ODD, D), 0.02),
        'xa_wq': nrm((DEPTH, D, D), D ** -0.5),
        'xa_wk': nrm((DEPTH, D, D), D ** -0.5),
        'xa_wv': nrm((DEPTH, D, D), D ** -0.5),
        'xa_wo': nrm((DEPTH, D, D), D ** -0.5),
        'ffn_wg': nrm((DEPTH, D, D_FF), D ** -0.5),
        'ffn_wu': nrm((DEPTH, D, D_FF), D ** -0.5),
        'ffn_wd': nrm((DEPTH, D_FF, D), D_FF ** -0.5),
    }


def reference(x_prompt, x_sample, cache_diff_k, cache_diff_v, state_pool, state_rw_shift, state_rw_wkv,
              cache_mem_k, cache_mem_v, mem_prompt, norm_mix_g, norm_xa_g, norm_ffn_g, final_norm_g,
              ev_w_in, ev_pool_w, ev_pool_scale, ev_lam_q1, ev_lam_k1, ev_lam_q2, ev_lam_k2, ev_subln_g,
              ev_w_out, rw_mu, rw_wr, rw_wk, rw_wv, rw_wo, rw_w0, rw_w1, rw_w2, rw_a0, rw_a1, rw_a2,
              rw_v0, rw_v1, rw_v2, rw_g1, rw_g2, rw_k_k, rw_k_a, rw_r_k, rw_lnx_g, rw_lnx_b,
              xa_wq, xa_wk, xa_wv, xa_wo, ffn_wg, ffn_wu, ffn_wd):
    P = dict(norm_mix_g=norm_mix_g, norm_xa_g=norm_xa_g, norm_ffn_g=norm_ffn_g, final_norm_g=final_norm_g,
             ev_w_in=ev_w_in, ev_pool_w=ev_pool_w, ev_pool_scale=ev_pool_scale, ev_lam_q1=ev_lam_q1,
             ev_lam_k1=ev_lam_k1, ev_lam_q2=ev_lam_q2, ev_lam_k2=ev_lam_k2, ev_subln_g=ev_subln_g,
             ev_w_out=ev_w_out, rw_mu=rw_mu, rw_wr=rw_wr, rw_wk=rw_wk, rw_wv=rw_wv, rw_wo=rw_wo,
             rw_w0=rw_w0, rw_w1=rw_w1, rw_w2=rw_w2, rw_a0=rw_a0, rw_a1=rw_a1, rw_a2=rw_a2,
             rw_v0=rw_v0, rw_v1=rw_v1, rw_v2=rw_v2, rw_g1=rw_g1, rw_g2=rw_g2, rw_k_k=rw_k_k,
             rw_k_a=rw_k_a, rw_r_k=rw_r_k, rw_lnx_g=rw_lnx_g, rw_lnx_b=rw_lnx_b, xa_wq=xa_wq,
             xa_wo=xa_wo, ffn_wg=ffn_wg, ffn_wu=ffn_wu, ffn_wd=ffn_wd)
    Bp, Tp = x_prompt.shape[:2]
    pos_p = jnp.arange(Tp, dtype=jnp.int32)
    p_mem_k = jnp.stack([(mem_prompt @ xa_wk[l]).reshape(Bp, -1, XA_HEADS, XA_DH) for l in range(DEPTH)])
    p_mem_v = jnp.stack([(mem_prompt @ xa_wv[l]).reshape(Bp, -1, XA_HEADS, XA_DH) for l in range(DEPTH)])
    zero_pool = jnp.zeros((N_EVEN, Bp, POOL_HIST, POOL_WIDTH), x_prompt.dtype)
    zero_shift = jnp.zeros((N_ODD, Bp, D_MODEL), x_prompt.dtype)
    zero_wkv = jnp.zeros((N_ODD, Bp, RW_HEADS, RW_N, RW_N), jnp.float32)
    y_prompt, p_diff_k, p_diff_v, p_pool, p_rw_shift, p_rw_wkv = trunk(
        x_prompt, pos_p, pos_p, None, None, zero_pool, zero_shift, zero_wkv, p_mem_k, p_mem_v, P)
    past = cache_diff_k.shape[2]
    Ts = x_sample.shape[1]
    pos_s = past + jnp.arange(Ts, dtype=jnp.int32)
    kpos_s = jnp.arange(past + Ts, dtype=jnp.int32)
    y_sample, s_diff_k, s_diff_v, s_pool, s_rw_shift, s_rw_wkv = trunk(
        x_sample, pos_s, kpos_s, cache_diff_k, cache_diff_v, state_pool, state_rw_shift, state_rw_wkv,
        cache_mem_k, cache_mem_v, P)
    return (y_prompt, y_sample, p_diff_k, p_diff_v, p_pool, p_rw_shift, p_rw_wkv, p_mem_k, p_mem_v,
            s_diff_k, s_diff_v, s_pool, s_rw_shift, s_rw_wkv)
```

```python
import functools
import math

import jax
import jax.numpy as jnp
from jax import lax
from jax.experimental import pallas as pl
from jax.experimental.pallas import tpu as pltpu

F32 = jnp.float32
BF16 = jnp.bfloat16

CHUNK = 64
POOL_GROUPS = 4
POOL_WINDOWS = (2, 4, 8, 16)
POOL_HIST = max(POOL_WINDOWS) - 1
POOL_HALO = 16
DIFF_DH = 64
DIFF_VD = 2 * DIFF_DH
RW_N = 64
RW_LN_EPS = 64e-5
XA_HEADS = 4
NORM_EPS = 1e-6
NEG_INF = -1e30
LANES = 128

VMEM_LIMIT = 56 * 1024 * 1024


def _cparams(sem):
    return pltpu.CompilerParams(dimension_semantics=sem, vmem_limit_bytes=VMEM_LIMIT)


def _const_spec(shape):
    nd = len(shape)
    return pl.BlockSpec(shape, lambda *_: (0,) * nd, pipeline_mode=pl.Buffered(1))


def _bdot(a, b):
    return jnp.dot(a.astype(BF16), b.astype(BF16), preferred_element_type=F32)


def _bdot_nt(a, b):
    return lax.dot_general(a.astype(BF16), b.astype(BF16), (((1,), (1,)), ((), ())),
                           preferred_element_type=F32)


def _bdot_tn(a, b):
    return lax.dot_general(a.astype(BF16), b.astype(BF16), (((0,), (0,)), ((), ())),
                           preferred_element_type=F32)


def _rms(x, g):
    return x * lax.rsqrt(jnp.mean(x * x, axis=-1, keepdims=True) + NORM_EPS) * g


def _sigmoid(x):
    return 1.0 / (1.0 + jnp.exp(-x))


def _log2(c):
    assert c > 0 and c & (c - 1) == 0, c
    return c.bit_length() - 1


def _div_pow2(x, c):
    return lax.shift_right_arithmetic(x, _log2(c))


def _mod_pow2(x, c):
    assert c & (c - 1) == 0, c
    return x & (c - 1)


def _memkv_kernel(x_ref, w_ref, o_ref, ob_ref):
    y = _bdot(x_ref[...], w_ref[0])
    o_ref[0] = y
    ob_ref[0] = y.astype(BF16)


def _mem_kv(mem2d, w_stack):
    n, d = mem2d.shape
    nw = w_stack.shape[0]
    return pl.pallas_call(
        _memkv_kernel,
        grid=(nw,),
        in_specs=[pl.BlockSpec((n, d), lambda i: (0, 0)),
                  pl.BlockSpec((1, d, d), lambda i: (i, 0, 0))],
        out_specs=[pl.BlockSpec((1, n, d), lambda i: (i, 0, 0)),
                   pl.BlockSpec((1, n, d), lambda i: (i, 0, 0))],
        out_shape=[jax.ShapeDtypeStruct((nw, n, d), F32),
                   jax.ShapeDtypeStruct((nw, n, d), BF16)],
        compiler_params=_cparams(("parallel",)),
        name="mem_kv",
    )(mem2d, w_stack)


def _pre_even_kernel(x_ref, g_ref, w_ref, u_ref, q_ref, k_ref, v_ref, kb_ref, vb_ref, *, pw, dw):
    h = _rms(x_ref[...], g_ref[...])
    z = _bdot(h, w_ref[...])
    u_ref[...] = z[:, :pw]
    q_ref[...] = (z[:, pw:pw + dw] * (DIFF_DH ** -0.5)).astype(BF16)
    k = z[:, pw + dw:pw + 2 * dw]
    v = z[:, pw + 2 * dw:]
    k_ref[...] = k
    v_ref[...] = v
    kb_ref[...] = k.astype(BF16)
    vb_ref[...] = v.astype(BF16)


def _pre_even(x2d, g, w_in, tm):
    n, d = x2d.shape
    pw = d // 2
    dw = d // 2
    row = lambda c: pl.BlockSpec((tm, c), lambda i: (i, 0))
    return pl.pallas_call(
        functools.partial(_pre_even_kernel, pw=pw, dw=dw),
        grid=(n // tm,),
        in_specs=[row(d), _const_spec((1, d)), _const_spec(w_in.shape)],
        out_specs=[row(pw), row(dw), row(dw), row(dw), row(dw), row(dw)],
        out_shape=[jax.ShapeDtypeStruct((n, pw), F32), jax.ShapeDtypeStruct((n, dw), BF16),
                   jax.ShapeDtypeStruct((n, dw), F32), jax.ShapeDtypeStruct((n, dw), F32),
                   jax.ShapeDtypeStruct((n, dw), BF16), jax.ShapeDtypeStruct((n, dw), BF16)],
        compiler_params=_cparams(("parallel",)),
        name="pre_even",
    )(x2d, g, w_in)


def _pool_kernel(u_ref, uprev_ref, hist_ref, w_ref, sc_ref, o_ref, *, tp, pos0, gw):
    t = pl.program_id(1)
    u = u_ref[0]
    prev = jnp.where(t == 0, hist_ref[0], uprev_ref[0])
    full = jnp.concatenate([prev, u], axis=0)
    sums = []
    s = full
    for sh in (1, 2, 4, 8):
        s = s + pltpu.roll(s, sh, 0)
        sums.append(s)
    pos = pos0 + t * tp + lax.broadcasted_iota(jnp.int32, (tp, 1), 0)
    for g, w in enumerate(POOL_WINDOWS):
        sl = slice(g * gw, (g + 1) * gw)
        cnt = jnp.minimum(pos + 1, w).astype(F32)
        pooled = sums[g][POOL_HALO:, sl] / cnt - u[:, sl]
        y = _bdot(pooled, w_ref[g]) * sc_ref[:, sl]
        o_ref[0, :, sl] = y.astype(BF16)


def _pool(u3, hist16, pool_w, scale, tp, pos0):
    b, t, pw = u3.shape
    gw = pw // POOL_GROUPS
    hb = tp // POOL_HALO
    return pl.pallas_call(
        functools.partial(_pool_kernel, tp=tp, pos0=pos0, gw=gw),
        grid=(b, t // tp),
        in_specs=[pl.BlockSpec((1, tp, pw), lambda i, j: (i, j, 0)),
                  pl.BlockSpec((1, POOL_HALO, pw), lambda i, j: (i, jnp.maximum(j * hb - 1, 0), 0)),
                  pl.BlockSpec((1, POOL_HALO, pw), lambda i, j: (i, 0, 0)),
                  _const_spec(pool_w.shape), _const_spec((1, pw))],
        out_specs=pl.BlockSpec((1, tp, pw), lambda i, j: (i, j, 0)),
        out_shape=jax.ShapeDtypeStruct((b, t, pw), BF16),
        compiler_params=_cparams(("parallel", "parallel")),
        name="pool_mixer",
    )(u3, u3, hist16, pool_w, scale)


def _lam(lq1, lk1, lq2, lk2, lam_init):
    return (jnp.exp(jnp.sum(lq1[...] * lk1[...], axis=-1, keepdims=True))
            - jnp.exp(jnp.sum(lq2[...] * lk2[...], axis=-1, keepdims=True)) + lam_init)


def _stack_maps(q):
    lane = lax.broadcasted_iota(jnp.int32, q.shape, 1)
    zero = jnp.zeros_like(q)
    return jnp.concatenate([jnp.where(lane < DIFF_DH, q, zero), jnp.where(lane >= DIFF_DH, q, zero)], axis=0)


def _softmax_step(s, v, m_prev, l_prev, acc_prev):
    m_new = jnp.maximum(m_prev, jnp.max(s, axis=-1, keepdims=True))
    alpha = jnp.exp(m_prev - m_new)
    p = jnp.exp(s - m_new)
    l_new = alpha * l_prev + jnp.sum(p, axis=-1, keepdims=True)
    acc_new = alpha * acc_prev + _bdot(p, v)
    return m_new, l_new, acc_new


def _diff_finish(l, acc, lam, subg, rows, out_scale):
    o = acc[:rows] / l[:rows] - lam * (acc[rows:] / l[rows:])
    o = o * lax.rsqrt(jnp.mean(o * o, axis=-1, keepdims=True) + NORM_EPS) * subg
    return o * out_scale


def _diffp_kernel(q_ref, k_ref, v_ref, lq1, lk1, lq2, lk2, subg_ref, o_ref, m_ref, l_ref, acc_ref,
                  *, tq, lam_init):
    i = pl.program_id(2)
    j = pl.program_id(3)

    @pl.when(j == 0)
    def _():
        m_ref[...] = jnp.full(m_ref.shape, NEG_INF, F32)
        l_ref[...] = jnp.zeros(l_ref.shape, F32)
        acc_ref[...] = jnp.zeros(acc_ref.shape, F32)

    def step(masked):
        q2 = _stack_maps(q_ref[0])
        s = _bdot_nt(q2, k_ref[0])
        if masked:
            rq = _mod_pow2(lax.broadcasted_iota(jnp.int32, s.shape, 0), tq)
            ck = lax.broadcasted_iota(jnp.int32, s.shape, 1)
            s = jnp.where(_div_pow2(ck, CHUNK) <= _div_pow2(rq, CHUNK), s, NEG_INF)
        m, l, acc = _softmax_step(s, v_ref[0], m_ref[...], l_ref[...], acc_ref[...])
        m_ref[...] = m
        l_ref[...] = l
        acc_ref[...] = acc

    @pl.when(j < i)
    def _():
        step(False)

    @pl.when(j == i)
    def _():
        step(True)
        lam = _lam(lq1, lk1, lq2, lk2, lam_init)
        o_ref[0] = _diff_finish(l_ref[...], acc_ref[...], lam, subg_ref[...], tq, 1.0 - lam_init).astype(BF16)


def _diff_prompt(q3, kb3, vb3, lam_vecs, subg, lam_init, tq):
    b, t, w = q3.shape
    heads = w // LANES
    nq = t // tq
    qspec = pl.BlockSpec((1, tq, LANES), lambda bb, h, i, j: (bb, i, h))
    kspec = pl.BlockSpec((1, tq, LANES), lambda bb, h, i, j: (bb, jnp.minimum(j, i), h))
    vec = lambda n: pl.BlockSpec((1, n), lambda *_: (0, 0))
    return pl.pallas_call(
        functools.partial(_diffp_kernel, tq=tq, lam_init=lam_init),
        grid=(b, heads, nq, nq),
        in_specs=[qspec, kspec, kspec, vec(DIFF_DH), vec(DIFF_DH), vec(DIFF_DH), vec(DIFF_DH), vec(DIFF_VD)],
        out_specs=qspec,
        out_shape=jax.ShapeDtypeStruct((b, t, w), BF16),
        scratch_shapes=[pltpu.VMEM((2 * tq, 1), F32), pltpu.VMEM((2 * tq, 1), F32),
                        pltpu.VMEM((2 * tq, LANES), F32)],
        compiler_params=_cparams(("parallel", "parallel", "parallel", "arbitrary")),
        name="diff_attn_prompt",
    )(q3, kb3, vb3, *lam_vecs, subg)


def _diffs_kernel(q_ref, ck_ref, cv_ref, kn_ref, vn_ref, lq1, lk1, lq2, lk2, subg_ref, o_ref,
                  m_ref, l_ref, acc_ref, *, ts, heads, lam_init):
    j = pl.program_id(1)
    nj = pl.num_programs(1)

    @pl.when(j == 0)
    def _():
        pad = jnp.zeros((LANES - ts, LANES), F32)
        col = lax.broadcasted_iota(jnp.int32, (2 * ts, LANES), 1)
        for h in range(heads):
            sl = slice(h * LANES, (h + 1) * LANES)
            q2 = _stack_maps(q_ref[0, :, sl])
            kn = jnp.concatenate([kn_ref[0, :, sl], pad], axis=0)
            vn = jnp.concatenate([vn_ref[0, :, sl], pad], axis=0)
            s = jnp.where(col < ts, _bdot_nt(q2, kn), NEG_INF)
            m0 = jnp.full((2 * ts, 1), NEG_INF, F32)
            m, l, acc = _softmax_step(s, vn, m0, jnp.zeros((2 * ts, 1), F32), jnp.zeros((2 * ts, LANES), F32))
            m_ref[h] = m
            l_ref[h] = l
            acc_ref[h] = acc

    for h in range(heads):
        sl = slice(h * LANES, (h + 1) * LANES)
        q2 = _stack_maps(q_ref[0, :, sl])
        s = _bdot_nt(q2, ck_ref[0, :, sl])
        m, l, acc = _softmax_step(s, cv_ref[0, :, sl], m_ref[h], l_ref[h], acc_ref[h])
        m_ref[h] = m
        l_ref[h] = l
        acc_ref[h] = acc

    @pl.when(j == nj - 1)
    def _():
        lam = _lam(lq1, lk1, lq2, lk2, lam_init)
        for h in range(heads):
            sl = slice(h * LANES, (h + 1) * LANES)
            o_ref[0, :, sl] = _diff_finish(l_ref[h], acc_ref[h], lam, subg_ref[...], ts,
                                           1.0 - lam_init).astype(BF16)


def _diff_sample(q3, ck3, cv3, kn3, vn3, lam_vecs, subg, lam_init, tk):
    b, ts, w = q3.shape
    past = ck3.shape[1]
    heads = w // LANES
    assert past % CHUNK == 0 and ts <= CHUNK and past % tk == 0
    new = pl.BlockSpec((1, ts, w), lambda bb, j: (bb, 0, 0))
    cache = pl.BlockSpec((1, tk, w), lambda bb, j: (bb, j, 0))
    vec = lambda n: pl.BlockSpec((1, n), lambda *_: (0, 0))
    return pl.pallas_call(
        functools.partial(_diffs_kernel, ts=ts, heads=heads, lam_init=lam_init),
        grid=(b, past // tk),
        in_specs=[new, cache, cache, new, new, vec(DIFF_DH), vec(DIFF_DH), vec(DIFF_DH), vec(DIFF_DH),
                  vec(DIFF_VD)],
        out_specs=new,
        out_shape=jax.ShapeDtypeStruct((b, ts, w), BF16),
        scratch_shapes=[pltpu.VMEM((heads, 2 * ts, 1), F32), pltpu.VMEM((heads, 2 * ts, 1), F32),
                        pltpu.VMEM((heads, 2 * ts, LANES), F32)],
        compiler_params=_cparams(("parallel", "arbitrary")),
        name="diff_attn_sample",
    )(q3, ck3, cv3, kn3, vn3, *lam_vecs, subg)


def _mixxa_kernel(x_ref, m1_ref, m2_ref, w1_ref, w2_ref, g_ref, wq_ref, mk_ref, mv_ref, wo_ref, o_ref,
                  *, nseq, heads):
    x1 = x_ref[...] + _bdot(m1_ref[...], w1_ref[...]) + _bdot(m2_ref[...], w2_ref[...])
    tm, d = x1.shape
    dh = d // heads
    h = _rms(x1, g_ref[...])
    q = (_bdot(h, wq_ref[...]) * (dh ** -0.5)).astype(BF16)
    q3 = q.reshape(nseq, tm // nseq, d)
    outs = []
    for hd in range(heads):
        sl = slice(hd * dh, (hd + 1) * dh)
        s = jnp.einsum("bqd,bkd->bqk", q3[:, :, sl], mk_ref[:, :, sl], preferred_element_type=F32)
        p = jnp.exp(s - jnp.max(s, axis=-1, keepdims=True))
        p = p / jnp.sum(p, axis=-1, keepdims=True)
        outs.append(jnp.einsum("bqk,bkd->bqd", p.astype(BF16), mv_ref[:, :, sl], preferred_element_type=F32))
    o = jnp.concatenate(outs, axis=-1).reshape(tm, d)
    o_ref[...] = x1 + _bdot(o, wo_ref[...])


def _mix_xa(x2d, m1, m2, w1, w2, g, wq, mk, mv, wo, tm, rows_per_seq):
    n, d = x2d.shape
    half = m1.shape[1]
    nmem = mk.shape[1]
    nseq = max(tm // rows_per_seq, 1)
    tiles_per_seq = max(rows_per_seq // tm, 1)
    row = lambda c: pl.BlockSpec((tm, c), lambda i: (i, 0))
    mem = pl.BlockSpec((nseq, nmem, d), lambda i: (i // tiles_per_seq, 0, 0))
    return pl.pallas_call(
        functools.partial(_mixxa_kernel, nseq=nseq, heads=XA_HEADS),
        grid=(n // tm,),
        in_specs=[row(d), row(half), row(half), _const_spec(w1.shape), _const_spec(w2.shape),
                  _const_spec((1, d)), _const_spec(wq.shape), mem, mem, _const_spec(wo.shape)],
        out_specs=row(d),
        out_shape=jax.ShapeDtypeStruct((n, d), F32),
        compiler_params=_cparams(("parallel",)),
        name="mix_xa",
    )(x2d, m1, m2, w1, w2, g, wq, mk, mv, wo)


def _ffn_kernel(x_ref, g_ref, wg_ref, wu_ref, wd_ref, gf_ref, o_ref, *, nchunk, final):
    x = x_ref[...]
    h = _rms(x, g_ref[...]).astype(BF16)
    ff = wg_ref.shape[1]
    fc = ff // nchunk
    acc = x
    for c in range(nchunk):
        sl = slice(c * fc, (c + 1) * fc)
        a = jnp.dot(h, wg_ref[:, sl], preferred_element_type=F32)
        b = jnp.dot(h, wu_ref[:, sl], preferred_element_type=F32)
        acc = acc + _bdot(a * _sigmoid(a) * b, wd_ref[sl, :])
    if final:
        acc = _rms(acc, gf_ref[...])
    o_ref[...] = acc


def _ffn(x2d, g, wg, wu, wd, gf, tm, final):
    n, d = x2d.shape
    row = pl.BlockSpec((tm, d), lambda i: (i, 0))
    return pl.pallas_call(
        functools.partial(_ffn_kernel, nchunk=2, final=final),
        grid=(n // tm,),
        in_specs=[row, _const_spec((1, d)), _const_spec(wg.shape), _const_spec(wu.shape),
                  _const_spec(wd.shape), _const_spec((1, d))],
        out_specs=row,
        out_shape=jax.ShapeDtypeStruct((n, d), F32),
        compiler_params=_cparams(("parallel",)),
        name="ffn",
    )(x2d, g, wg, wu, wd, gf)


def _pre_rwkv_kernel(*refs, tm, rows_per_seq, hl_rows, has_vres):
    (x_ref, xp_ref, first_ref, g_ref, mu_ref, wr_ref, wk_ref, wv_ref, w0_ref, w1_ref, w2_ref,
     a0_ref, a1_ref, a2_ref, g1_ref, g2_ref, kk_ref, ka_ref) = refs[:18]
    pos = 18
    if has_vres:
        vf_ref, v0_ref, v1_ref, v2_ref = refs[pos:pos + 4]
        pos += 4
    r_ref, k_ref, v_ref, kr_ref, ag_ref, w_ref, gate_ref, hl_ref = refs[pos:]

    i = pl.program_id(0)
    g = g_ref[...]
    h = _rms(x_ref[...], g)
    hp_last = _rms(xp_ref[...], g)[xp_ref.shape[0] - 1:, :]
    rowl = lax.broadcasted_iota(jnp.int32, (tm, 1), 0)
    shifted = jnp.where(rowl == 0, hp_last, pltpu.roll(h, 1, 0))
    is_start = _mod_pow2(i * tm + rowl, rows_per_seq) == 0
    h_prev = jnp.where(is_start, first_ref[...], shifted)
    xx = h_prev - h
    mix = lambda n: (h + xx * mu_ref[n:n + 1, :]).astype(BF16)
    xr, xw, xk, xv, xa, xg = (mix(n) for n in range(6))

    r_ref[...] = _bdot(xr, wr_ref[...])
    z = w0_ref[...] + _bdot(jnp.tanh(_bdot(xw, w1_ref[...])), w2_ref[...])
    w_ref[...] = -(jnp.maximum(-z, 0.0) + jnp.log1p(jnp.exp(-jnp.abs(z)))) - 0.5
    k = _bdot(xk, wk_ref[...])
    v = _bdot(xv, wv_ref[...])
    if has_vres:
        v = v + (vf_ref[...] - v) * _sigmoid(v0_ref[...] + _bdot(_bdot(xv, v1_ref[...]), v2_ref[...]))
    v_ref[...] = v
    a = _sigmoid(a0_ref[...] + _bdot(_bdot(xa, a1_ref[...]), a2_ref[...]))
    ag_ref[...] = a
    gate_ref[...] = _bdot(_sigmoid(_bdot(xg, g1_ref[...])), g2_ref[...])
    kr_ref[...] = k * kk_ref[...]
    k_ref[...] = k * (1.0 + (a - 1.0) * ka_ref[...])
    hl_ref[...] = h[tm - hl_rows:, :]


def _pre_rwkv(x2d, first, p, vres, tm, rows_per_seq, hl_rows):
    n, d = x2d.shape
    has_vres = vres is not None
    xp_rows = 8
    row = pl.BlockSpec((tm, d), lambda i: (i, 0))
    vec = _const_spec((1, d))
    ins = [x2d, x2d, first, p["g"], p["mu"], p["wr"], p["wk"], p["wv"], p["w0"], p["w1"], p["w2"],
           p["a0"], p["a1"], p["a2"], p["g1"], p["g2"], p["k_k"], p["k_a"]]
    specs = [row,
             pl.BlockSpec((xp_rows, d), lambda i: (jnp.maximum(i * (tm // xp_rows) - 1, 0), 0)),
             pl.BlockSpec((tm, d), lambda i: (i // max(rows_per_seq // tm, 1), 0)),
             vec, _const_spec(p["mu"].shape), _const_spec(p["wr"].shape), _const_spec(p["wk"].shape),
             _const_spec(p["wv"].shape), vec, _const_spec(p["w1"].shape), _const_spec(p["w2"].shape),
             vec, _const_spec(p["a1"].shape), _const_spec(p["a2"].shape), _const_spec(p["g1"].shape),
             _const_spec(p["g2"].shape), vec, vec]
    if has_vres:
        vf, v0, v1, v2 = vres
        ins += [vf, v0, v1, v2]
        specs += [row, vec, _const_spec(v1.shape), _const_spec(v2.shape)]
    nt = n // tm
    return pl.pallas_call(
        functools.partial(_pre_rwkv_kernel, tm=tm, rows_per_seq=rows_per_seq, hl_rows=hl_rows,
                          has_vres=has_vres),
        grid=(nt,),
        in_specs=specs,
        out_specs=[row] * 7 + [pl.BlockSpec((hl_rows, d), lambda i: (i, 0))],
        out_shape=[jax.ShapeDtypeStruct((n, d), F32)] * 7 + [jax.ShapeDtypeStruct((nt * hl_rows, d), F32)],
        compiler_params=_cparams(("parallel",)),
        name="pre_rwkv",
    )(*ins)


def _split3(x):
    x1 = x.astype(BF16)
    r1 = x - x1.astype(F32)
    x2 = r1.astype(BF16)
    x3 = (r1 - x2.astype(F32)).astype(BF16)
    return x1, x2, x3


def _head_sum(x, lo_mask):
    s_lo = jnp.sum(jnp.where(lo_mask, x, 0.0), axis=-1, keepdims=True)
    s_hi = jnp.sum(jnp.where(lo_mask, 0.0, x), axis=-1, keepdims=True)
    return jnp.where(lo_mask, s_lo, s_hi)


def _rwkv_rec_kernel(r_ref, k_ref, v_ref, kr_ref, ag_ref, w_ref, gate_ref, lng_ref, lnb_ref, rk_ref, s0_ref,
                     y_ref, sfin_ref, s_scr, *, L, pairs):
    c = pl.program_id(2)
    nc = pl.num_programs(2)

    @pl.when(c == 0)
    def _():
        s_scr[...] = s0_ref[0]

    L2 = 2 * L
    sub = min(16, L)
    lane = lax.broadcasted_iota(jnp.int32, (L, LANES), 1)
    lo = lane < RW_N
    row2 = lax.broadcasted_iota(jnp.int32, (L2, L2), 0)
    col2 = lax.broadcasted_iota(jnp.int32, (L2, L2), 1)
    same = _div_pow2(row2, L) == _div_pow2(col2, L)
    strict = same & (col2 < row2)
    incl = same & (col2 <= row2)
    diag_blk = _div_pow2(row2, sub) == _div_pow2(col2, sub)
    eye = (row2 == col2).astype(F32)
    tri = (lax.broadcasted_iota(jnp.int32, (L, L), 1) <= lax.broadcasted_iota(jnp.int32, (L, L), 0)).astype(BF16)

    def stack(x):
        xb = x.astype(BF16)
        zero = jnp.zeros_like(xb)
        return jnp.concatenate([jnp.where(lo, xb, zero), jnp.where(lo, zero, xb)], axis=0)

    for pr in range(pairs):
        sl = slice(pr * LANES, (pr + 1) * LANES)
        r = r_ref[0, :, sl]
        k = k_ref[0, :, sl]
        v = v_ref[0, :, sl]
        kr = kr_ref[0, :, sl]
        ag = ag_ref[0, :, sl]
        logd = -jnp.exp(w_ref[0, :, sl])

        kk = kr / jnp.maximum(jnp.sqrt(_head_sum(kr * kr, lo)), 1e-12)
        l1, l2, l3 = _split3(logd)
        dot = lambda a, b: jnp.dot(a, b, preferred_element_type=F32)
        cum = dot(tri, l1) + dot(tri, l2) + dot(tri, l3)
        e_pos = jnp.exp(cum)
        e_neg = jnp.exp(-cum)
        p_last = e_pos[L - 1:, :]
        a_s = stack(-kk * jnp.exp(cum - logd))
        r_s = stack(r * e_pos)
        bt = kk * ag * e_neg
        kt = k * e_neg
        b_s = stack(bt)
        k_s = stack(kt)
        v_s = stack(v)
        bh_s = stack(bt * p_last)
        kh_s = stack(kt * p_last)

        g_all = _bdot_nt(jnp.concatenate([a_s, r_s], axis=0), jnp.concatenate([b_s, k_s], axis=0))
        a_ab = jnp.where(strict, g_all[:L2, :L2], 0.0)
        a_ak = jnp.where(strict, g_all[:L2, L2:], 0.0)
        a_rb = jnp.where(incl, g_all[L2:, :L2], 0.0)
        a_rk = jnp.where(incl, g_all[L2:, L2:], 0.0)

        s_bd = s_scr[pr]
        wmat = _bdot_nt(a_s, s_bd) + _bdot(a_ak, v_s)

        a_d = jnp.where(diag_blk, a_ab, 0.0)
        pm = eye + a_d
        pw = a_d
        for _ in range(int(math.log2(sub)) - 1):
            pw = _bdot(pw, pw)
            pm = pm + _bdot(pw, pm)
        x = _bdot(pm, wmat)
        nblk = L // sub
        if nblk > 1:
            nm = _bdot(pm, a_ab - a_d)
            fac = eye + nm
            pwn = nm
            for _ in range(int(math.log2(nblk)) - 1):
                pwn = _bdot(pwn, pwn)
                fac = fac + _bdot(fac, pwn)
            x = _bdot(fac, x)
        u = x

        y2 = _bdot_nt(r_s, s_bd) + _bdot(a_rb, u) + _bdot(a_rk, v_s)
        y = y2[:L] + y2[L:]
        s_scr[pr] = s_bd * p_last + _bdot_tn(u, bh_s) + _bdot_tn(v_s, kh_s)

        mean = _head_sum(y, lo) * (1.0 / RW_N)
        yc = y - mean
        var = _head_sum(yc * yc, lo) * (1.0 / RW_N)
        yn = yc * lax.rsqrt(var + RW_LN_EPS) * lng_ref[:, sl] + lnb_ref[:, sl]
        bonus = _head_sum(r * k * rk_ref[:, sl], lo) * v
        y_ref[0, :, sl] = ((yn + bonus) * gate_ref[0, :, sl]).astype(BF16)

    @pl.when(c == nc - 1)
    def _():
        sfin_ref[0] = s_scr[...]


def _rwkv_rec(arrs, lng, lnb, rk, s0_bd, L, pairs):
    b, t, d = arrs[0].shape
    npair = d // LANES
    groups = npair // pairs
    w = pairs * LANES
    tok = pl.BlockSpec((1, L, w), lambda bb, gg, c: (bb, c, gg))
    vec = pl.BlockSpec((1, w), lambda bb, gg, c: (0, gg))
    st = pl.BlockSpec((1, pairs, LANES, LANES), lambda bb, gg, c: (bb, gg, 0, 0))
    return pl.pallas_call(
        functools.partial(_rwkv_rec_kernel, L=L, pairs=pairs),
        grid=(b, groups, t // L),
        in_specs=[tok] * 7 + [vec, vec, vec, st],
        out_specs=[tok, st],
        out_shape=[jax.ShapeDtypeStruct((b, t, d), BF16),
                   jax.ShapeDtypeStruct((b, npair, LANES, LANES), F32)],
        scratch_shapes=[pltpu.VMEM((pairs, LANES, LANES), F32)],
        compiler_params=_cparams(("parallel", "parallel", "arbitrary")),
        name="rwkv_rec",
    )(*arrs, lng, lnb, rk, s0_bd)


def _state_to_blockdiag(s):
    b, h, n, _ = s.shape
    s5 = s.reshape(b, h // 2, 2, n, n)
    eye2 = jnp.eye(2, dtype=s.dtype)
    return jnp.einsum("bphvk,hg->bphvgk", s5, eye2).reshape(b, h // 2, 2 * n, 2 * n)


def _blockdiag_to_state(sbd, n):
    b, p = sbd.shape[:2]
    s6 = sbd.reshape(b, p, 2, n, 2, n)
    return jnp.stack([s6[:, :, 0, :, 0, :], s6[:, :, 1, :, 1, :]], axis=2).reshape(b, 2 * p, n, n)


def _trunk(x, pos0, diff_k_past, diff_v_past, pool_hist, rw_shift, rw_state, mem_kb, mem_vb, W, cfg):
    b, t, d = x.shape
    n = b * t
    depth = W["norm_mix_g"].shape[0]
    tm, tm_rw, tm_xa = cfg["tm"], cfg["tm_rw"], cfg["tm_xa"]
    x2 = x.reshape(n, d)
    new_k, new_v, new_pool, new_shift, new_s = [], [], [], [], []
    v_first = None
    half = d // 2
    for l in range(depth):
        if l % 2 == 0:
            e = l // 2
            u, q, k, v, kb, vb = _pre_even(x2, W["norm_mix_g"][l], W["ev_w_in"][e], tm)
            new_k.append(k.reshape(b, t, -1, 2 * DIFF_DH))
            new_v.append(v.reshape(b, t, -1, DIFF_VD))
            u3 = u.reshape(b, t, half)
            new_pool.append(u3[:, t - POOL_HIST:])
            hist16 = jnp.pad(pool_hist[e], ((0, 0), (POOL_HALO - POOL_HIST, 0), (0, 0)))
            m1 = _pool(u3, hist16, W["ev_pool_w"][e], W["ev_pool_scale"][e], cfg["tp"], pos0).reshape(n, half)
            lam_init = 0.8 - 0.6 * math.exp(-0.3 * l)
            lam_vecs = [W[nm][e] for nm in ("ev_lam_q1", "ev_lam_k1", "ev_lam_q2", "ev_lam_k2")]
            q3 = q.reshape(b, t, half)
            if diff_k_past is None:
                m2 = _diff_prompt(q3, kb.reshape(b, t, half), vb.reshape(b, t, half), lam_vecs,
                                  W["ev_subln_g"][e], lam_init, cfg["tq"])
            else:
                past = diff_k_past.shape[2]
                m2 = _diff_sample(q3, diff_k_past[e].reshape(b, past, half), diff_v_past[e].reshape(b, past, half),
                                  k.reshape(b, t, half), v.reshape(b, t, half), lam_vecs,
                                  W["ev_subln_g"][e], lam_init, cfg["tk"])
            m2 = m2.reshape(n, half)
            w_o = W["ev_w_out"][e]
        else:
            o = l // 2
            p = {nm: W["rw_" + nm][o] for nm in ("mu", "wr", "wk", "wv", "w0", "w1", "w2", "a0", "a1", "a2",
                                                   "g1", "g2", "k_k", "k_a")}
            p["g"] = W["norm_mix_g"][l]
            vres = None if o == 0 else (v_first, W["rw_v0"][o - 1], W["rw_v1"][o - 1], W["rw_v2"][o - 1])
            rows_first = jnp.broadcast_to(rw_shift[o][:, None, :], (b, min(t, tm_rw), d)).reshape(-1, d)
            outs = _pre_rwkv(x2, rows_first, p, vres, tm_rw, t, cfg["hl_rows"])
            r, k, v, kr, ag, wl, gate, hl = outs
            if o == 0:
                v_first = v
            if cfg["hl_rows"] == tm_rw:
                new_shift.append(hl.reshape(b, t, d)[:, -1])
            else:
                per_seq = t // tm_rw
                new_shift.append(hl.reshape(b, per_seq, cfg["hl_rows"], d)[:, -1, -1])
            arrs = [a.reshape(b, t, d) for a in (r, k, v, kr, ag, wl, gate)]
            y, s_fin = _rwkv_rec(arrs, W["rw_lnx_g"][o], W["rw_lnx_b"][o], W["rw_r_k"][o],
                                 _state_to_blockdiag(rw_state[o]), cfg["L"], cfg["pairs"])
            new_s.append(_blockdiag_to_state(s_fin, RW_N))
            y2 = y.reshape(n, d)
            m1, m2 = y2[:, :half], y2[:, half:]
            w_o = W["rw_wo"][o]
        x2 = _mix_xa(x2, m1, m2, w_o[:half], w_o[half:], W["norm_xa_g"][l], W["xa_wq"][l], mem_kb[l], mem_vb[l],
                     W["xa_wo"][l], tm_xa, t)
        x2 = _ffn(x2, W["norm_ffn_g"][l], W["ffn_wg"][l], W["ffn_wu"][l], W["ffn_wd"][l], W["final_norm_g"],
                  tm, final=(l == depth - 1))
    return (x2.reshape(b, t, d), jnp.stack(new_k), jnp.stack(new_v), jnp.stack(new_pool),
            jnp.stack(new_shift), jnp.stack(new_s))


def kernel(x_prompt, x_sample, cache_diff_k, cache_diff_v, state_pool, state_rw_shift, state_rw_wkv, cache_mem_k, cache_mem_v, mem_prompt, norm_mix_g, norm_xa_g, norm_ffn_g, final_norm_g, ev_w_in, ev_pool_w, ev_pool_scale, ev_lam_q1, ev_lam_k1, ev_lam_q2, ev_lam_k2, ev_subln_g, ev_w_out, rw_mu, rw_wr, rw_wk, rw_wv, rw_wo, rw_w0, rw_w1, rw_w2, rw_a0, rw_a1, rw_a2, rw_v0, rw_v1, rw_v2, rw_g1, rw_g2, rw_k_k, rw_k_a, rw_r_k, rw_lnx_g, rw_lnx_b, xa_wq, xa_wk, xa_wv, xa_wo, ffn_wg, ffn_wu, ffn_wd):
    bf = lambda a: a.astype(BF16)
    vec = lambda a: a.reshape(a.shape[0], 1, -1)
    W = dict(
        norm_mix_g=vec(norm_mix_g), norm_xa_g=vec(norm_xa_g), norm_ffn_g=vec(norm_ffn_g),
        final_norm_g=final_norm_g.reshape(1, -1),
        ev_w_in=bf(ev_w_in), ev_pool_w=bf(ev_pool_w), ev_pool_scale=vec(ev_pool_scale),
        ev_lam_q1=vec(ev_lam_q1), ev_lam_k1=vec(ev_lam_k1), ev_lam_q2=vec(ev_lam_q2), ev_lam_k2=vec(ev_lam_k2),
        ev_subln_g=vec(ev_subln_g), ev_w_out=bf(ev_w_out),
        rw_mu=rw_mu, rw_wr=bf(rw_wr), rw_wk=bf(rw_wk), rw_wv=bf(rw_wv), rw_wo=bf(rw_wo),
        rw_w0=vec(rw_w0), rw_w1=bf(rw_w1), rw_w2=bf(rw_w2), rw_a0=vec(rw_a0), rw_a1=bf(rw_a1), rw_a2=bf(rw_a2),
        rw_v0=vec(rw_v0), rw_v1=bf(rw_v1), rw_v2=bf(rw_v2), rw_g1=bf(rw_g1), rw_g2=bf(rw_g2),
        rw_k_k=vec(rw_k_k), rw_k_a=vec(rw_k_a), rw_r_k=rw_r_k.reshape(rw_r_k.shape[0], 1, -1),
        rw_lnx_g=vec(rw_lnx_g), rw_lnx_b=vec(rw_lnx_b),
        xa_wq=bf(xa_wq), xa_wo=bf(xa_wo), ffn_wg=bf(ffn_wg), ffn_wu=bf(ffn_wu), ffn_wd=bf(ffn_wd),
    )
    depth, d = norm_mix_g.shape
    bp, tp_len = x_prompt.shape[:2]
    bs, ts = x_sample.shape[:2]
    n_even = ev_w_in.shape[0]
    n_odd = rw_wr.shape[0]
    nmem = mem_prompt.shape[1]

    kv_f32, kv_bf = _mem_kv(mem_prompt.reshape(bp * nmem, d), bf(jnp.concatenate([xa_wk, xa_wv], axis=0)))
    p_mem_k = kv_f32[:depth].reshape(depth, bp, nmem, XA_HEADS, d // XA_HEADS)
    p_mem_v = kv_f32[depth:].reshape(depth, bp, nmem, XA_HEADS, d // XA_HEADS)
    pmk = kv_bf[:depth].reshape(depth, bp, nmem, d)
    pmv = kv_bf[depth:].reshape(depth, bp, nmem, d)

    cfg_p = dict(tm=512, tm_rw=256, tm_xa=512, tp=512, tq=512, tk=None, L=64, pairs=4, hl_rows=8)
    zero_pool = jnp.zeros((n_even, bp, POOL_HIST, d // 2), F32)
    zero_shift = jnp.zeros((n_odd, bp, d), F32)
    zero_wkv = jnp.zeros((n_odd, bp, d // RW_N, RW_N, RW_N), F32)
    outs_p = _trunk(x_prompt, 0, None, None, zero_pool, zero_shift, zero_wkv, pmk, pmv, W, cfg_p)

    past = cache_diff_k.shape[2]
    cfg_s = dict(tm=bs * ts, tm_rw=bs * ts, tm_xa=128, tp=ts, tq=None, tk=1024, L=ts, pairs=8,
                 hl_rows=bs * ts)
    smk = bf(cache_mem_k).reshape(depth, bs, nmem, d)
    smv = bf(cache_mem_v).reshape(depth, bs, nmem, d)
    outs_s = _trunk(x_sample, past, cache_diff_k, cache_diff_v, state_pool, state_rw_shift, state_rw_wkv,
                    smk, smv, W, cfg_s)

    y_p, pk, pv, pp, psh, pS = outs_p
    y_s, sk, sv, sp, ssh, sS = outs_s
    return (y_p, y_s, pk, pv, pp, psh, pS, p_mem_k, p_mem_v, sk, sv, sp, ssh, sS)
```

```python
import functools
import math

import jax
import jax.numpy as jnp
from jax import lax
from jax.experimental import pallas as pl
from jax.experimental.pallas import tpu as pltpu

F32 = jnp.float32
BF16 = jnp.bfloat16

CHUNK = 64
POOL_GROUPS = 4
POOL_WINDOWS = (2, 4, 8, 16)
POOL_HIST = max(POOL_WINDOWS) - 1
POOL_HALO = 16
DIFF_DH = 64
DIFF_VD = 2 * DIFF_DH
RW_N = 64
RW_LN_EPS = 64e-5
XA_HEADS = 4
NORM_EPS = 1e-6
NEG_INF = -1e30
LANES = 128

VMEM_LIMIT = 56 * 1024 * 1024


def _cparams(sem):
    return pltpu.CompilerParams(dimension_semantics=sem, vmem_limit_bytes=VMEM_LIMIT)


def _const_spec(shape):
    nd = len(shape)
    return pl.BlockSpec(shape, lambda *_: (0,) * nd, pipeline_mode=pl.Buffered(1))


def _bdot(a, b):
    return jnp.dot(a.astype(BF16), b.astype(BF16), preferred_element_type=F32)


def _bdot_nt(a, b):
    return lax.dot_general(a.astype(BF16), b.astype(BF16), (((1,), (1,)), ((), ())),
                           preferred_element_type=F32)


def _bdot_tn(a, b):
    return lax.dot_general(a.astype(BF16), b.astype(BF16), (((0,), (0,)), ((), ())),
                           preferred_element_type=F32)


def _rms(x, g):
    return x * lax.rsqrt(jnp.mean(x * x, axis=-1, keepdims=True) + NORM_EPS) * g


def _sigmoid(x):
    return 1.0 / (1.0 + jnp.exp(-x))


def _log2(c):
    assert c > 0 and c & (c - 1) == 0, c
    return c.bit_length() - 1


def _div_pow2(x, c):
    return lax.shift_right_arithmetic(x, _log2(c))


def _mod_pow2(x, c):
    assert c & (c - 1) == 0, c
    return x & (c - 1)


def _memkv_kernel(x_ref, w_ref, o_ref, ob_ref):
    y = _bdot(x_ref[...], w_ref[0])
    o_ref[0] = y
    ob_ref[0] = y.astype(BF16)


def _mem_kv(mem2d, w_stack):
    n, d = mem2d.shape
    nw = w_stack.shape[0]
    return pl.pallas_call(
        _memkv_kernel,
        grid=(nw,),
        in_specs=[pl.BlockSpec((n, d), lambda i: (0, 0)),
                  pl.BlockSpec((1, d, d), lambda i: (i, 0, 0))],
        out_specs=[pl.BlockSpec((1, n, d), lambda i: (i, 0, 0)),
                   pl.BlockSpec((1, n, d), lambda i: (i, 0, 0))],
        out_shape=[jax.ShapeDtypeStruct((nw, n, d), F32),
                   jax.ShapeDtypeStruct((nw, n, d), BF16)],
        compiler_params=_cparams(("parallel",)),
        name="mem_kv",
    )(mem2d, w_stack)


def _pre_even_kernel(x_ref, g_ref, w_ref, u_ref, k_ref, v_ref, *rest, pw, dw, transposed):
    h = _rms(x_ref[...], g_ref[...])
    z = _bdot(h, w_ref[...])
    u_ref[...] = z[:, :pw]
    q = z[:, pw:pw + dw] * (DIFF_DH ** -0.5)
    k = z[:, pw + dw:pw + 2 * dw]
    v = z[:, pw + 2 * dw:]
    k_ref[...] = k
    v_ref[...] = v
    if transposed:
        kb_ref, qt_ref, vt_ref = rest
        kb_ref[...] = k.astype(BF16)
        for hd in range(dw // LANES):
            sl = slice(hd * LANES, (hd + 1) * LANES)
            qt_ref[0, hd] = q[:, sl].T.astype(BF16)
            vt_ref[0, hd] = v[:, sl].T.astype(BF16)
    else:
        (q_ref,) = rest
        q_ref[...] = q.astype(BF16)


def _pre_even(x2d, g, w_in, tm, seq_len, transposed):
    n, d = x2d.shape
    pw = d // 2
    dw = d // 2
    heads = dw // LANES
    row = lambda c: pl.BlockSpec((tm, c), lambda i: (i, 0))
    out_specs = [row(pw), row(dw), row(dw)]
    out_shape = [jax.ShapeDtypeStruct((n, pw), F32), jax.ShapeDtypeStruct((n, dw), F32),
                 jax.ShapeDtypeStruct((n, dw), F32)]
    if transposed:
        tps = seq_len // tm
        tspec = pl.BlockSpec((1, heads, LANES, tm), lambda i: (i // tps, 0, 0, i % tps))
        tshape = jax.ShapeDtypeStruct((n // seq_len, heads, LANES, seq_len), BF16)
        out_specs += [row(dw), tspec, tspec]
        out_shape += [jax.ShapeDtypeStruct((n, dw), BF16), tshape, tshape]
    else:
        out_specs += [row(dw)]
        out_shape += [jax.ShapeDtypeStruct((n, dw), BF16)]
    return pl.pallas_call(
        functools.partial(_pre_even_kernel, pw=pw, dw=dw, transposed=transposed),
        grid=(n // tm,),
        in_specs=[row(d), _const_spec((1, d)), _const_spec(w_in.shape)],
        out_specs=out_specs,
        out_shape=out_shape,
        compiler_params=_cparams(("parallel",)),
        name="pre_even",
    )(x2d, g, w_in)


def _pool_kernel(u_ref, uprev_ref, hist_ref, w_ref, sc_ref, o_ref, *, tp, pos0, gw):
    t = pl.program_id(1)
    u = u_ref[0]
    prev = jnp.where(t == 0, hist_ref[0], uprev_ref[0])
    full = jnp.concatenate([prev, u], axis=0)
    sums = []
    s = full
    for sh in (1, 2, 4, 8):
        s = s + pltpu.roll(s, sh, 0)
        sums.append(s)
    pos = pos0 + t * tp + lax.broadcasted_iota(jnp.int32, (tp, 1), 0)
    for g, w in enumerate(POOL_WINDOWS):
        sl = slice(g * gw, (g + 1) * gw)
        cnt = jnp.minimum(pos + 1, w).astype(F32)
        pooled = sums[g][POOL_HALO:, sl] / cnt - u[:, sl]
        y = _bdot(pooled, w_ref[g]) * sc_ref[:, sl]
        o_ref[0, :, sl] = y.astype(BF16)


def _pool(u3, hist16, pool_w, scale, tp, pos0):
    b, t, pw = u3.shape
    gw = pw // POOL_GROUPS
    hb = tp // POOL_HALO
    return pl.pallas_call(
        functools.partial(_pool_kernel, tp=tp, pos0=pos0, gw=gw),
        grid=(b, t // tp),
        in_specs=[pl.BlockSpec((1, tp, pw), lambda i, j: (i, j, 0)),
                  pl.BlockSpec((1, POOL_HALO, pw), lambda i, j: (i, jnp.maximum(j * hb - 1, 0), 0)),
                  pl.BlockSpec((1, POOL_HALO, pw), lambda i, j: (i, 0, 0)),
                  _const_spec(pool_w.shape), _const_spec((1, pw))],
        out_specs=pl.BlockSpec((1, tp, pw), lambda i, j: (i, j, 0)),
        out_shape=jax.ShapeDtypeStruct((b, t, pw), BF16),
        compiler_params=_cparams(("parallel", "parallel")),
        name="pool_mixer",
    )(u3, u3, hist16, pool_w, scale)


def _lam(lq1, lk1, lq2, lk2, lam_init):
    return (jnp.exp(jnp.sum(lq1[...] * lk1[...], axis=-1, keepdims=True))
            - jnp.exp(jnp.sum(lq2[...] * lk2[...], axis=-1, keepdims=True)) + lam_init)


def _stack_maps(q):
    lane = lax.broadcasted_iota(jnp.int32, q.shape, 1)
    zero = jnp.zeros_like(q)
    return jnp.concatenate([jnp.where(lane < DIFF_DH, q, zero), jnp.where(lane >= DIFF_DH, q, zero)], axis=0)


def _softmax_step(s, v, m_prev, l_prev, acc_prev):
    m_new = jnp.maximum(m_prev, jnp.max(s, axis=-1, keepdims=True))
    alpha = jnp.exp(m_prev - m_new)
    p = jnp.exp(s - m_new)
    l_new = alpha * l_prev + jnp.sum(p, axis=-1, keepdims=True)
    acc_new = alpha * acc_prev + _bdot(p, v)
    return m_new, l_new, acc_new


def _diff_finish(l, acc, lam, subg, rows, out_scale):
    o = acc[:rows] / l[:rows] - lam * (acc[rows:] / l[rows:])
    o = o * lax.rsqrt(jnp.mean(o * o, axis=-1, keepdims=True) + NORM_EPS) * subg
    return o * out_scale


def _diffp_kernel(qt_ref, k_ref, vt_ref, lq1, lk1, lq2, lk2, subg_ref, o_ref, q2_ref, m_ref, l_ref, acc_ref,
                  *, tq, lam_init):
    i = pl.program_id(2)
    j = pl.program_id(3)

    @pl.when(j == 0)
    def _():
        qt = qt_ref[0, 0]
        row = lax.broadcasted_iota(jnp.int32, qt.shape, 0)
        zero = jnp.zeros_like(qt)
        q2_ref[:, :tq] = jnp.where(row < DIFF_DH, qt, zero)
        q2_ref[:, tq:] = jnp.where(row >= DIFF_DH, qt, zero)
        m_ref[...] = jnp.full(m_ref.shape, NEG_INF, F32)
        l_ref[...] = jnp.zeros(l_ref.shape, F32)
        acc_ref[...] = jnp.zeros(acc_ref.shape, F32)

    def step(masked):
        s = jnp.dot(k_ref[0], q2_ref[...], preferred_element_type=F32)
        if masked:
            kc = _div_pow2(lax.broadcasted_iota(jnp.int32, s.shape, 0), CHUNK)
            qc = _div_pow2(_mod_pow2(lax.broadcasted_iota(jnp.int32, s.shape, 1), tq), CHUNK)
            s = jnp.where(kc <= qc, s, NEG_INF)
        m_prev = m_ref[...]
        m_new = jnp.maximum(m_prev, jnp.max(s, axis=0, keepdims=True))
        alpha = jnp.exp(m_prev - m_new)
        p = jnp.exp(s - m_new)
        l_ref[...] = alpha * l_ref[...] + jnp.sum(p, axis=0, keepdims=True)
        acc_ref[...] = alpha * acc_ref[...] + jnp.dot(vt_ref[0, 0], p.astype(BF16), preferred_element_type=F32)
        m_ref[...] = m_new

    @pl.when(j < i)
    def _():
        step(False)

    @pl.when(j == i)
    def _():
        step(True)
        lam = _lam(lq1, lk1, lq2, lk2, lam_init)
        acc = acc_ref[...]
        l = l_ref[...]
        ot = acc[:, :tq] / l[:, :tq] - lam * (acc[:, tq:] / l[:, tq:])
        ot = ot * lax.rsqrt(jnp.mean(ot * ot, axis=0, keepdims=True) + NORM_EPS)
        o_ref[0] = (ot.T * subg_ref[...] * (1.0 - lam_init)).astype(BF16)


def _diff_prompt(qt, kb3, vt, lam_vecs, subg, lam_init, tq):
    b, t, w = kb3.shape
    heads = w // LANES
    nq = t // tq
    qspec = pl.BlockSpec((1, 1, LANES, tq), lambda bb, h, i, j: (bb, h, 0, i))
    kspec = pl.BlockSpec((1, tq, LANES), lambda bb, h, i, j: (bb, jnp.minimum(j, i), h))
    vspec = pl.BlockSpec((1, 1, LANES, tq), lambda bb, h, i, j: (bb, h, 0, jnp.minimum(j, i)))
    vec = lambda n: pl.BlockSpec((1, n), lambda *_: (0, 0))
    return pl.pallas_call(
        functools.partial(_diffp_kernel, tq=tq, lam_init=lam_init),
        grid=(b, heads, nq, nq),
        in_specs=[qspec, kspec, vspec, vec(DIFF_DH), vec(DIFF_DH), vec(DIFF_DH), vec(DIFF_DH), vec(DIFF_VD)],
        out_specs=pl.BlockSpec((1, tq, LANES), lambda bb, h, i, j: (bb, i, h)),
        out_shape=jax.ShapeDtypeStruct((b, t, w), BF16),
        scratch_shapes=[pltpu.VMEM((LANES, 2 * tq), BF16), pltpu.VMEM((1, 2 * tq), F32),
                        pltpu.VMEM((1, 2 * tq), F32), pltpu.VMEM((LANES, 2 * tq), F32)],
        compiler_params=_cparams(("parallel", "parallel", "parallel", "arbitrary")),
        name="diff_attn_prompt",
    )(qt, kb3, vt, *lam_vecs, subg)


def _diffs_kernel(q_ref, ck_ref, cv_ref, kn_ref, vn_ref, lq1, lk1, lq2, lk2, subg_ref, o_ref,
                  m_ref, l_ref, acc_ref, *, ts, heads, lam_init):
    j = pl.program_id(1)
    nj = pl.num_programs(1)
    nq = heads * 2 * ts
    q2 = jnp.concatenate([_stack_maps(q_ref[0, :, h * LANES:(h + 1) * LANES]) for h in range(heads)], axis=0)

    def visible(ncols, valid_cols):
        qh = _div_pow2(lax.broadcasted_iota(jnp.int32, (nq, ncols), 0), 2 * ts)
        col = lax.broadcasted_iota(jnp.int32, (nq, ncols), 1)
        ok = _mod_pow2(col, heads) == qh
        return ok if valid_cols is None else ok & (col < valid_cols)

    @pl.when(j == 0)
    def _():
        nnew = kn_ref.shape[1]
        pad = jnp.zeros((LANES - nnew, LANES), F32)
        kn = jnp.concatenate([kn_ref[0], pad], axis=0)
        vn = jnp.concatenate([vn_ref[0], pad], axis=0)
        s = jnp.where(visible(LANES, nnew), _bdot_nt(q2, kn), NEG_INF)
        m, l, acc = _softmax_step(s, vn, jnp.full((nq, 1), NEG_INF, F32), jnp.zeros((nq, 1), F32),
                                  jnp.zeros((nq, LANES), F32))
        m_ref[...] = m
        l_ref[...] = l
        acc_ref[...] = acc

    s = _bdot_nt(q2, ck_ref[...])
    s = jnp.where(visible(s.shape[1], None), s, NEG_INF)
    m, l, acc = _softmax_step(s, cv_ref[...], m_ref[...], l_ref[...], acc_ref[...])
    m_ref[...] = m
    l_ref[...] = l
    acc_ref[...] = acc

    @pl.when(j == nj - 1)
    def _():
        lam = _lam(lq1, lk1, lq2, lk2, lam_init)
        l_all = l_ref[...]
        acc_all = acc_ref[...]
        for h in range(heads):
            rows = slice(h * 2 * ts, (h + 1) * 2 * ts)
            o_ref[0, :, h * LANES:(h + 1) * LANES] = _diff_finish(
                l_all[rows], acc_all[rows], lam, subg_ref[...], ts, 1.0 - lam_init).astype(BF16)


def _diff_sample(q3, ck5, cv5, layer, kn3, vn3, lam_vecs, subg, lam_init, tk):
    b, ts, w = q3.shape
    nl, _, past, heads, _ = ck5.shape
    assert heads * LANES == w and heads * ts <= LANES
    assert past % CHUNK == 0 and ts <= CHUNK and past % tk == 0
    rows = lambda a: a.reshape(a.shape[:-3] + (a.shape[-3] * heads, LANES))
    qspec = pl.BlockSpec((1, ts, w), lambda bb, j: (bb, 0, 0))
    new = pl.BlockSpec((1, ts * heads, LANES), lambda bb, j: (bb, 0, 0))
    cache = pl.BlockSpec((None, None, tk * heads, LANES), lambda bb, j: (layer, bb, j, 0))
    vec = lambda n: pl.BlockSpec((1, n), lambda *_: (0, 0))
    nq = heads * 2 * ts
    return pl.pallas_call(
        functools.partial(_diffs_kernel, ts=ts, heads=heads, lam_init=lam_init),
        grid=(b, past // tk),
        in_specs=[qspec, cache, cache, new, new, vec(DIFF_DH), vec(DIFF_DH), vec(DIFF_DH), vec(DIFF_DH),
                  vec(DIFF_VD)],
        out_specs=qspec,
        out_shape=jax.ShapeDtypeStruct((b, ts, w), BF16),
        scratch_shapes=[pltpu.VMEM((nq, 1), F32), pltpu.VMEM((nq, 1), F32), pltpu.VMEM((nq, LANES), F32)],
        compiler_params=_cparams(("parallel", "arbitrary")),
        name="diff_attn_sample",
    )(q3, rows(ck5), rows(cv5), rows(kn3.reshape(b, ts, heads, LANES)), rows(vn3.reshape(b, ts, heads, LANES)),
      *lam_vecs, subg)


def _mixxa_kernel(x_ref, m1_ref, m2_ref, w1_ref, w2_ref, g_ref, wq_ref, mk_ref, mv_ref, wo_ref, o_ref,
                  *, nseq, heads):
    x1 = x_ref[...] + _bdot(m1_ref[...], w1_ref[...]) + _bdot(m2_ref[...], w2_ref[...])
    tm, d = x1.shape
    dh = d // heads
    h = _rms(x1, g_ref[...])
    q = (_bdot(h, wq_ref[...]) * (dh ** -0.5)).astype(BF16)
    q3 = q.reshape(nseq, tm // nseq, d)
    outs = []
    for hd in range(heads):
        sl = slice(hd * dh, (hd + 1) * dh)
        s = jnp.einsum("bqd,bkd->bqk", q3[:, :, sl], mk_ref[:, :, sl], preferred_element_type=F32)
        p = jnp.exp(s - jnp.max(s, axis=-1, keepdims=True))
        p = p / jnp.sum(p, axis=-1, keepdims=True)
        outs.append(jnp.einsum("bqk,bkd->bqd", p.astype(BF16), mv_ref[:, :, sl], preferred_element_type=F32))
    o = jnp.concatenate(outs, axis=-1).reshape(tm, d)
    o_ref[...] = x1 + _bdot(o, wo_ref[...])


def _mix_xa(x2d, m1, m2, w1, w2, g, wq, mk, mv, layer, wo, tm, rows_per_seq):
    n, d = x2d.shape
    half = d // 2
    nseq = max(tm // rows_per_seq, 1)
    tiles_per_seq = max(rows_per_seq // tm, 1)
    row = lambda c: pl.BlockSpec((tm, c), lambda i: (i, 0))
    col = lambda c: pl.BlockSpec((tm, half), lambda i: (i, c))
    mem = pl.BlockSpec((None, nseq) + mk.shape[2:], lambda i: (layer, i // tiles_per_seq, 0, 0))
    return pl.pallas_call(
        functools.partial(_mixxa_kernel, nseq=nseq, heads=XA_HEADS),
        grid=(n // tm,),
        in_specs=[row(d), col(m1[1]), col(m2[1]), _const_spec(w1.shape), _const_spec(w2.shape),
                  _const_spec((1, d)), _const_spec(wq.shape), mem, mem, _const_spec(wo.shape)],
        out_specs=row(d),
        out_shape=jax.ShapeDtypeStruct((n, d), F32),
        compiler_params=_cparams(("parallel",)),
        name="mix_xa",
    )(x2d, m1[0], m2[0], w1, w2, g, wq, mk, mv, wo)


def _ffn_kernel(x_ref, g_ref, wg_ref, wu_ref, wd_ref, gf_ref, o_ref, *, nchunk, final):
    x = x_ref[...]
    h = _rms(x, g_ref[...]).astype(BF16)
    ff = wg_ref.shape[1]
    fc = ff // nchunk
    acc = x
    for c in range(nchunk):
        sl = slice(c * fc, (c + 1) * fc)
        a = jnp.dot(h, wg_ref[:, sl], preferred_element_type=F32)
        b = jnp.dot(h, wu_ref[:, sl], preferred_element_type=F32)
        acc = acc + _bdot(a * _sigmoid(a) * b, wd_ref[sl, :])
    if final:
        acc = _rms(acc, gf_ref[...])
    o_ref[...] = acc


def _ffn(x2d, g, wg, wu, wd, gf, tm, final):
    n, d = x2d.shape
    row = pl.BlockSpec((tm, d), lambda i: (i, 0))
    return pl.pallas_call(
        functools.partial(_ffn_kernel, nchunk=2, final=final),
        grid=(n // tm,),
        in_specs=[row, _const_spec((1, d)), _const_spec(wg.shape), _const_spec(wu.shape),
                  _const_spec(wd.shape), _const_spec((1, d))],
        out_specs=row,
        out_shape=jax.ShapeDtypeStruct((n, d), F32),
        compiler_params=_cparams(("parallel",)),
        name="ffn",
    )(x2d, g, wg, wu, wd, gf)


def _pre_rwkv_kernel(*refs, tm, rows_per_seq, hl_rows, has_vres):
    (x_ref, xp_ref, first_ref, g_ref, mu_ref, wr_ref, wk_ref, wv_ref, w0_ref, w1_ref, w2_ref,
     a0_ref, a1_ref, a2_ref, g1_ref, g2_ref, kk_ref, ka_ref) = refs[:18]
    pos = 18
    if has_vres:
        vf_ref, v0_ref, v1_ref, v2_ref = refs[pos:pos + 4]
        pos += 4
    r_ref, k_ref, v_ref, kr_ref, ag_ref, w_ref, gate_ref, hl_ref = refs[pos:]

    i = pl.program_id(0)
    g = g_ref[...]
    h = _rms(x_ref[...], g)
    hp_last = _rms(xp_ref[...], g)[xp_ref.shape[0] - 1:, :]
    rowl = lax.broadcasted_iota(jnp.int32, (tm, 1), 0)
    shifted = jnp.where(rowl == 0, hp_last, pltpu.roll(h, 1, 0))
    is_start = _mod_pow2(i * tm + rowl, rows_per_seq) == 0
    h_prev = jnp.where(is_start, first_ref[...], shifted)
    xx = h_prev - h
    mix = lambda n: (h + xx * mu_ref[n:n + 1, :]).astype(BF16)
    xr, xw, xk, xv, xa, xg = (mix(n) for n in range(6))

    r_ref[...] = _bdot(xr, wr_ref[...])
    z = w0_ref[...] + _bdot(jnp.tanh(_bdot(xw, w1_ref[...])), w2_ref[...])
    w_ref[...] = -(jnp.maximum(-z, 0.0) + jnp.log1p(jnp.exp(-jnp.abs(z)))) - 0.5
    k = _bdot(xk, wk_ref[...])
    v = _bdot(xv, wv_ref[...])
    if has_vres:
        v = v + (vf_ref[...] - v) * _sigmoid(v0_ref[...] + _bdot(_bdot(xv, v1_ref[...]), v2_ref[...]))
    v_ref[...] = v
    a = _sigmoid(a0_ref[...] + _bdot(_bdot(xa, a1_ref[...]), a2_ref[...]))
    ag_ref[...] = a
    gate_ref[...] = _bdot(_sigmoid(_bdot(xg, g1_ref[...])), g2_ref[...])
    kr_ref[...] = k * kk_ref[...]
    k_ref[...] = k * (1.0 + (a - 1.0) * ka_ref[...])
    hl_ref[...] = h[tm - hl_rows:, :]


def _pre_rwkv(x2d, first, p, vres, tm, rows_per_seq, hl_rows):
    n, d = x2d.shape
    has_vres = vres is not None
    xp_rows = 8
    row = pl.BlockSpec((tm, d), lambda i: (i, 0))
    vec = _const_spec((1, d))
    ins = [x2d, x2d, first, p["g"], p["mu"], p["wr"], p["wk"], p["wv"], p["w0"], p["w1"], p["w2"],
           p["a0"], p["a1"], p["a2"], p["g1"], p["g2"], p["k_k"], p["k_a"]]
    specs = [row,
             pl.BlockSpec((xp_rows, d), lambda i: (jnp.maximum(i * (tm // xp_rows) - 1, 0), 0)),
             pl.BlockSpec((tm, d), lambda i: (i // max(rows_per_seq // tm, 1), 0)),
             vec, _const_spec(p["mu"].shape), _const_spec(p["wr"].shape), _const_spec(p["wk"].shape),
             _const_spec(p["wv"].shape), vec, _const_spec(p["w1"].shape), _const_spec(p["w2"].shape),
             vec, _const_spec(p["a1"].shape), _const_spec(p["a2"].shape), _const_spec(p["g1"].shape),
             _const_spec(p["g2"].shape), vec, vec]
    if has_vres:
        vf, v0, v1, v2 = vres
        ins += [vf, v0, v1, v2]
        specs += [row, vec, _const_spec(v1.shape), _const_spec(v2.shape)]
    nt = n // tm
    return pl.pallas_call(
        functools.partial(_pre_rwkv_kernel, tm=tm, rows_per_seq=rows_per_seq, hl_rows=hl_rows,
                          has_vres=has_vres),
        grid=(nt,),
        in_specs=specs,
        out_specs=[row] * 7 + [pl.BlockSpec((hl_rows, d), lambda i: (i, 0))],
        out_shape=[jax.ShapeDtypeStruct((n, d), F32)] * 7 + [jax.ShapeDtypeStruct((nt * hl_rows, d), F32)],
        compiler_params=_cparams(("parallel",)),
        name="pre_rwkv",
    )(*ins)


def _split3(x):
    x1 = x.astype(BF16)
    r1 = x - x1.astype(F32)
    x2 = r1.astype(BF16)
    x3 = (r1 - x2.astype(F32)).astype(BF16)
    return x1, x2, x3


def _head_sum(x, lo_mask):
    s_lo = jnp.sum(jnp.where(lo_mask, x, 0.0), axis=-1, keepdims=True)
    s_hi = jnp.sum(jnp.where(lo_mask, 0.0, x), axis=-1, keepdims=True)
    return jnp.where(lo_mask, s_lo, s_hi)


def _rwkv_rec_kernel(r_ref, k_ref, v_ref, kr_ref, ag_ref, w_ref, gate_ref, lng_ref, lnb_ref, rk_ref, s0_ref,
                     y_ref, sfin_ref, s_scr, *, L, pairs):
    c = pl.program_id(2)
    nc = pl.num_programs(2)
    P = range(pairs)

    @pl.when(c == 0)
    def _():
        zero = jnp.zeros((RW_N, RW_N), F32)
        for p in P:
            top = jnp.concatenate([s0_ref[0, 2 * p], zero], axis=1)
            bot = jnp.concatenate([zero, s0_ref[0, 2 * p + 1]], axis=1)
            s_scr[p] = jnp.concatenate([top, bot], axis=0)

    L2 = 2 * L
    sub = min(16, L)
    nblk = L // sub
    lane = lax.broadcasted_iota(jnp.int32, (L, LANES), 1)
    lo = lane < RW_N
    row2 = lax.broadcasted_iota(jnp.int32, (L2, L2), 0)
    col2 = lax.broadcasted_iota(jnp.int32, (L2, L2), 1)
    same = _div_pow2(row2, L) == _div_pow2(col2, L)
    strict = same & (col2 < row2)
    incl = same & (col2 <= row2)
    diag_blk = _div_pow2(row2, sub) == _div_pow2(col2, sub)
    eye = (row2 == col2).astype(F32)
    tri = (lax.broadcasted_iota(jnp.int32, (L, L), 1) <= lax.broadcasted_iota(jnp.int32, (L, L), 0)).astype(BF16)

    def stack(x):
        xb = x.astype(BF16)
        zero = jnp.zeros_like(xb)
        return jnp.concatenate([jnp.where(lo, xb, zero), jnp.where(lo, zero, xb)], axis=0)

    dot = lambda a, b: jnp.dot(a, b, preferred_element_type=F32)

    sl = [slice(p * LANES, (p + 1) * LANES) for p in P]
    r = [r_ref[0, :, sl[p]] for p in P]
    k = [k_ref[0, :, sl[p]] for p in P]
    v = [v_ref[0, :, sl[p]] for p in P]
    logd = [-jnp.exp(w_ref[0, :, sl[p]]) for p in P]
    kk = []
    for p in P:
        kr = kr_ref[0, :, sl[p]]
        kk.append(kr / jnp.maximum(jnp.sqrt(_head_sum(kr * kr, lo)), 1e-12))
    cum = []
    for p in P:
        l1, l2, l3 = _split3(logd[p])
        cum.append(dot(tri, l1) + dot(tri, l2) + dot(tri, l3))
    ar_s, bk_s, v_s, bkh_s, p_last = [], [], [], [], []
    for p in P:
        e_pos = jnp.exp(cum[p])
        e_neg = jnp.exp(-cum[p])
        pl_ = e_pos[L - 1:, :]
        bt = kk[p] * ag_ref[0, :, sl[p]] * e_neg
        kt = k[p] * e_neg
        ar_s.append(jnp.concatenate([stack(-kk[p] * jnp.exp(cum[p] - logd[p])), stack(r[p] * e_pos)], axis=0))
        bk_s.append(jnp.concatenate([stack(bt), stack(kt)], axis=0))
        bkh_s.append(jnp.concatenate([stack(bt * pl_), stack(kt * pl_)], axis=0))
        v_s.append(stack(v[p]))
        p_last.append(pl_)
    s_bd = [s_scr[p] for p in P]
    ss = [_bdot_nt(ar_s[p], s_bd[p]) for p in P]
    g_all = [_bdot_nt(ar_s[p], bk_s[p]) for p in P]
    a_ab = [jnp.where(strict, g_all[p][:L2, :L2], 0.0) for p in P]
    a_akrk = [jnp.concatenate([jnp.where(strict, g_all[p][:L2, L2:], 0.0),
                               jnp.where(incl, g_all[p][L2:, L2:], 0.0)], axis=0) for p in P]
    a_rb = [jnp.where(incl, g_all[p][L2:, :L2], 0.0) for p in P]
    sv = [_bdot(a_akrk[p], v_s[p]) for p in P]
    wmat = [ss[p][:L2] + sv[p][:L2] for p in P]

    a_d = [jnp.where(diag_blk, a_ab[p], 0.0) for p in P]
    pm = [eye + a_d[p] for p in P]
    pw = a_d
    for _ in range(_log2(sub) - 1):
        pw = [_bdot(pw[p], pw[p]) for p in P]
        pm = [pm[p] + _bdot(pw[p], pm[p]) for p in P]
    if nblk > 1:
        xn = [_bdot(pm[p], jnp.concatenate([wmat[p], a_ab[p] - a_d[p]], axis=1)) for p in P]
        u = [xn[p][:, :LANES] for p in P]
        nm = [xn[p][:, LANES:] for p in P]
        steps = _log2(nblk)
        for it in range(steps):
            u = [u[p] + _bdot(nm[p], u[p]) for p in P]
            if it < steps - 1:
                nm = [_bdot(nm[p], nm[p]) for p in P]
    else:
        u = [_bdot(pm[p], wmat[p]) for p in P]

    y2 = [ss[p][L2:] + sv[p][L2:] + _bdot(a_rb[p], u[p]) for p in P]
    for p in P:
        uv = jnp.concatenate([u[p].astype(BF16), v_s[p]], axis=0)
        s_scr[p] = s_bd[p] * p_last[p] + _bdot_tn(uv, bkh_s[p])
    for p in P:
        y = y2[p][:L] + y2[p][L:]
        mean = _head_sum(y, lo) * (1.0 / RW_N)
        yc = y - mean
        var = _head_sum(yc * yc, lo) * (1.0 / RW_N)
        yn = yc * lax.rsqrt(var + RW_LN_EPS) * lng_ref[:, sl[p]] + lnb_ref[:, sl[p]]
        bonus = _head_sum(r[p] * k[p] * rk_ref[:, sl[p]], lo) * v[p]
        y_ref[0, :, sl[p]] = ((yn + bonus) * gate_ref[0, :, sl[p]]).astype(BF16)

    @pl.when(c == nc - 1)
    def _():
        for p in P:
            s = s_scr[p]
            sfin_ref[0, 2 * p] = s[:RW_N, :RW_N]
            sfin_ref[0, 2 * p + 1] = s[RW_N:, RW_N:]


def _rwkv_rec(arrs, lng, lnb, rk, s0, L, pairs):
    b, t, d = arrs[0].shape
    heads = d // RW_N
    groups = heads // (2 * pairs)
    w = pairs * LANES
    tok = pl.BlockSpec((1, L, w), lambda bb, gg, c: (bb, c, gg))
    vec = pl.BlockSpec((1, w), lambda bb, gg, c: (0, gg))
    st = pl.BlockSpec((1, 2 * pairs, RW_N, RW_N), lambda bb, gg, c: (bb, gg, 0, 0))
    return pl.pallas_call(
        functools.partial(_rwkv_rec_kernel, L=L, pairs=pairs),
        grid=(b, groups, t // L),
        in_specs=[tok] * 7 + [vec, vec, vec, st],
        out_specs=[tok, st],
        out_shape=[jax.ShapeDtypeStruct((b, t, d), BF16),
                   jax.ShapeDtypeStruct((b, heads, RW_N, RW_N), F32)],
        scratch_shapes=[pltpu.VMEM((pairs, LANES, LANES), F32)],
        compiler_params=_cparams(("parallel", "parallel", "arbitrary")),
        name="rwkv_rec",
    )(*arrs, lng, lnb, rk, s0)


def _trunk(x, pos0, diff_k_past, diff_v_past, pool_hist, rw_shift, rw_state, mem_k, mem_v, W, cfg):
    b, t, d = x.shape
    n = b * t
    depth = W["norm_mix_g"].shape[0]
    tm, tm_rw, tm_xa = cfg["tm"], cfg["tm_rw"], cfg["tm_xa"]
    x2 = x.reshape(n, d)
    new_k, new_v, new_pool, new_shift, new_s = [], [], [], [], []
    v_first = None
    half = d // 2
    for l in range(depth):
        if l % 2 == 0:
            e = l // 2
            prompt = diff_k_past is None
            outs = _pre_even(x2, W["norm_mix_g"][l], W["ev_w_in"][e], tm, t, transposed=prompt)
            u, k, v = outs[:3]
            new_k.append(k.reshape(b, t, -1, 2 * DIFF_DH))
            new_v.append(v.reshape(b, t, -1, DIFF_VD))
            u3 = u.reshape(b, t, half)
            new_pool.append(u3[:, t - POOL_HIST:])
            hist16 = jnp.pad(pool_hist[e], ((0, 0), (POOL_HALO - POOL_HIST, 0), (0, 0)))
            m1 = _pool(u3, hist16, W["ev_pool_w"][e], W["ev_pool_scale"][e], cfg["tp"], pos0).reshape(n, half)
            lam_init = 0.8 - 0.6 * math.exp(-0.3 * l)
            lam_vecs = [W[nm][e] for nm in ("ev_lam_q1", "ev_lam_k1", "ev_lam_q2", "ev_lam_k2")]
            if prompt:
                kb, qt, vt = outs[3:]
                m2 = _diff_prompt(qt, kb.reshape(b, t, half), vt, lam_vecs, W["ev_subln_g"][e], lam_init,
                                  cfg["tq"])
            else:
                m2 = _diff_sample(outs[3].reshape(b, t, half), diff_k_past, diff_v_past, e,
                                  k.reshape(b, t, half), v.reshape(b, t, half), lam_vecs,
                                  W["ev_subln_g"][e], lam_init, cfg["tk"])
            m1, m2 = (m1, 0), (m2.reshape(n, half), 0)
            w_o = W["ev_w_out"][e]
        else:
            o = l // 2
            p = {nm: W["rw_" + nm][o] for nm in ("mu", "wr", "wk", "wv", "w0", "w1", "w2", "a0", "a1", "a2",
                                                   "g1", "g2", "k_k", "k_a")}
            p["g"] = W["norm_mix_g"][l]
            vres = None if o == 0 else (v_first, W["rw_v0"][o - 1], W["rw_v1"][o - 1], W["rw_v2"][o - 1])
            rows_first = jnp.broadcast_to(rw_shift[o][:, None, :], (b, min(t, tm_rw), d)).reshape(-1, d)
            outs = _pre_rwkv(x2, rows_first, p, vres, tm_rw, t, cfg["hl_rows"])
            r, k, v, kr, ag, wl, gate, hl = outs
            if o == 0:
                v_first = v
            if cfg["hl_rows"] == tm_rw:
                new_shift.append(hl.reshape(b, t, d)[:, -1])
            else:
                per_seq = t // tm_rw
                new_shift.append(hl.reshape(b, per_seq, cfg["hl_rows"], d)[:, -1, -1])
            arrs = [a.reshape(b, t, d) for a in (r, k, v, kr, ag, wl, gate)]
            y, s_fin = _rwkv_rec(arrs, W["rw_lnx_g"][o], W["rw_lnx_b"][o], W["rw_r_k"][o],
                                 rw_state[o], cfg["L"], cfg["pairs"])
            new_s.append(s_fin)
            y2 = y.reshape(n, d)
            m1, m2 = (y2, 0), (y2, 1)
            w_o = W["rw_wo"][o]
        x2 = _mix_xa(x2, m1, m2, w_o[:half], w_o[half:], W["norm_xa_g"][l], W["xa_wq"][l], mem_k, mem_v, l,
                     W["xa_wo"][l], tm_xa, t)
        x2 = _ffn(x2, W["norm_ffn_g"][l], W["ffn_wg"][l], W["ffn_wu"][l], W["ffn_wd"][l], W["final_norm_g"],
                  tm, final=(l == depth - 1))
    return (x2.reshape(b, t, d), jnp.stack(new_k), jnp.stack(new_v), jnp.stack(new_pool),
            jnp.stack(new_shift), jnp.stack(new_s))


def kernel(x_prompt, x_sample, cache_diff_k, cache_diff_v, state_pool, state_rw_shift, state_rw_wkv, cache_mem_k, cache_mem_v, mem_prompt, norm_mix_g, norm_xa_g, norm_ffn_g, final_norm_g, ev_w_in, ev_pool_w, ev_pool_scale, ev_lam_q1, ev_lam_k1, ev_lam_q2, ev_lam_k2, ev_subln_g, ev_w_out, rw_mu, rw_wr, rw_wk, rw_wv, rw_wo, rw_w0, rw_w1, rw_w2, rw_a0, rw_a1, rw_a2, rw_v0, rw_v1, rw_v2, rw_g1, rw_g2, rw_k_k, rw_k_a, rw_r_k, rw_lnx_g, rw_lnx_b, xa_wq, xa_wk, xa_wv, xa_wo, ffn_wg, ffn_wu, ffn_wd):
    bf = lambda a: a.astype(BF16)
    vec = lambda a: a.reshape(a.shape[0], 1, -1)
    W = dict(
        norm_mix_g=vec(norm_mix_g), norm_xa_g=vec(norm_xa_g), norm_ffn_g=vec(norm_ffn_g),
        final_norm_g=final_norm_g.reshape(1, -1),
        ev_w_in=bf(ev_w_in), ev_pool_w=bf(ev_pool_w), ev_pool_scale=vec(ev_pool_scale),
        ev_lam_q1=vec(ev_lam_q1), ev_lam_k1=vec(ev_lam_k1), ev_lam_q2=vec(ev_lam_q2), ev_lam_k2=vec(ev_lam_k2),
        ev_subln_g=vec(ev_subln_g), ev_w_out=bf(ev_w_out),
        rw_mu=rw_mu, rw_wr=bf(rw_wr), rw_wk=bf(rw_wk), rw_wv=bf(rw_wv), rw_wo=bf(rw_wo),
        rw_w0=vec(rw_w0), rw_w1=bf(rw_w1), rw_w2=bf(rw_w2), rw_a0=vec(rw_a0), rw_a1=bf(rw_a1), rw_a2=bf(rw_a2),
        rw_v0=vec(rw_v0), rw_v1=bf(rw_v1), rw_v2=bf(rw_v2), rw_g1=bf(rw_g1), rw_g2=bf(rw_g2),
        rw_k_k=vec(rw_k_k), rw_k_a=vec(rw_k_a), rw_r_k=rw_r_k.reshape(rw_r_k.shape[0], 1, -1),
        rw_lnx_g=vec(rw_lnx_g), rw_lnx_b=vec(rw_lnx_b),
        xa_wq=bf(xa_wq), xa_wo=bf(xa_wo), ffn_wg=bf(ffn_wg), ffn_wu=bf(ffn_wu), ffn_wd=bf(ffn_wd),
    )
    depth, d = norm_mix_g.shape
    bp, tp_len = x_prompt.shape[:2]
    bs, ts = x_sample.shape[:2]
    n_even = ev_w_in.shape[0]
    n_odd = rw_wr.shape[0]
    nmem = mem_prompt.shape[1]

    kv_f32, kv_bf = _mem_kv(mem_prompt.reshape(bp * nmem, d), bf(jnp.concatenate([xa_wk, xa_wv], axis=0)))
    p_mem_k = kv_f32[:depth].reshape(depth, bp, nmem, XA_HEADS, d // XA_HEADS)
    p_mem_v = kv_f32[depth:].reshape(depth, bp, nmem, XA_HEADS, d // XA_HEADS)
    pmk = kv_bf[:depth].reshape(depth, bp, nmem, d)
    pmv = kv_bf[depth:].reshape(depth, bp, nmem, d)

    cfg_p = dict(tm=512, tm_rw=256, tm_xa=512, tp=512, tq=512, tk=None, L=64, pairs=8, hl_rows=8)
    zero_pool = jnp.zeros((n_even, bp, POOL_HIST, d // 2), F32)
    zero_shift = jnp.zeros((n_odd, bp, d), F32)
    zero_wkv = jnp.zeros((n_odd, bp, d // RW_N, RW_N, RW_N), F32)
    outs_p = _trunk(x_prompt, 0, None, None, zero_pool, zero_shift, zero_wkv, pmk, pmv, W, cfg_p)

    past = cache_diff_k.shape[2]
    cfg_s = dict(tm=bs * ts, tm_rw=bs * ts, tm_xa=8 * ts, tp=ts, tq=None, tk=1024, L=ts, pairs=8,
                 hl_rows=bs * ts)
    smk = bf(cache_mem_k).reshape(depth, bs, nmem, d)
    smv = bf(cache_mem_v).reshape(depth, bs, nmem, d)
    outs_s = _trunk(x_sample, past, cache_diff_k, cache_diff_v, state_pool, state_rw_shift, state_rw_wkv,
                    smk, smv, W, cfg_s)

    y_p, pk, pv, pp, psh, pS = outs_p
    y_s, sk, sv, sp, ssh, sS = outs_s
    return (y_p, y_s, pk, pv, pp, psh, pS, p_mem_k, p_mem_v, sk, sv, sp, ssh, sS)
```

```python
import functools
import math

import jax
import jax.numpy as jnp
import numpy as np
from jax import lax
from jax.experimental import pallas as pl
from jax.experimental.pallas import tpu as pltpu

F32 = jnp.float32
BF16 = jnp.bfloat16

CHUNK = 64
POOL_GROUPS = 4
POOL_WINDOWS = (2, 4, 8, 16)
POOL_HIST = max(POOL_WINDOWS) - 1
POOL_HALO = 16
DIFF_DH = 64
DIFF_VD = 2 * DIFF_DH
RW_N = 64
RW_LN_EPS = 64e-5
XA_HEADS = 4
NORM_EPS = 1e-6
NEG_INF = -1e30
LANES = 128

VMEM_LIMIT = 56 * 1024 * 1024


def _cparams(sem):
    return pltpu.CompilerParams(dimension_semantics=sem, vmem_limit_bytes=VMEM_LIMIT)


def _layer_spec(arr, layer, block=None, index=None):
    block = tuple(arr.shape[1:]) if block is None else tuple(block)
    index = (0,) * len(block) if index is None else tuple(index)
    return pl.BlockSpec((None,) + block, lambda *_: (layer,) + index, pipeline_mode=pl.Buffered(1))


def _bdot(a, b):
    return jnp.dot(a.astype(BF16), b.astype(BF16), preferred_element_type=F32)


def _bdot_nt(a, b):
    return lax.dot_general(a.astype(BF16), b.astype(BF16), (((1,), (1,)), ((), ())),
                           preferred_element_type=F32)


def _bdot_tn(a, b):
    return lax.dot_general(a.astype(BF16), b.astype(BF16), (((0,), (0,)), ((), ())),
                           preferred_element_type=F32)


def _rms(x, g):
    return x * lax.rsqrt(jnp.mean(x * x, axis=-1, keepdims=True) + NORM_EPS) * g


def _sigmoid(x):
    return 1.0 / (1.0 + jnp.exp(-x))


def _log2(c):
    assert c > 0 and c & (c - 1) == 0, c
    return c.bit_length() - 1


def _div_pow2(x, c):
    return lax.shift_right_arithmetic(x, _log2(c))


def _mod_pow2(x, c):
    assert c & (c - 1) == 0, c
    return x & (c - 1)


def _memkv_kernel(x_ref, w_ref, o_ref, ob_ref):
    y = _bdot(x_ref[...], w_ref[0])
    o_ref[0] = y
    ob_ref[0] = y.astype(BF16)


def _mem_kv(mem2d, w_stack):
    n, d = mem2d.shape
    nw = w_stack.shape[0]
    return pl.pallas_call(
        _memkv_kernel,
        grid=(nw,),
        in_specs=[pl.BlockSpec((n, d), lambda i: (0, 0)),
                  pl.BlockSpec((1, d, d), lambda i: (i, 0, 0))],
        out_specs=[pl.BlockSpec((1, n, d), lambda i: (i, 0, 0)),
                   pl.BlockSpec((1, n, d), lambda i: (i, 0, 0))],
        out_shape=[jax.ShapeDtypeStruct((nw, n, d), F32),
                   jax.ShapeDtypeStruct((nw, n, d), BF16)],
        compiler_params=_cparams(("parallel",)),
        name="mem_kv",
    )(mem2d, w_stack)


def _pre_even_kernel(x_ref, g_ref, w_ref, u_ref, k_ref, v_ref, *rest, pw, dw, transposed):
    h = _rms(x_ref[...], g_ref[...])
    z = _bdot(h, w_ref[...])
    tm = z.shape[0]
    heads = dw // LANES
    u_ref[...] = z[:, :pw]
    q = z[:, pw:pw + dw] * (DIFF_DH ** -0.5)
    k = z[:, pw + dw:pw + 2 * dw]
    v = z[:, pw + 2 * dw:]
    for hd in range(heads):
        sl = slice(hd * LANES, (hd + 1) * LANES)
        k_ref[pl.ds(hd, tm, stride=heads), :] = k[:, sl]
        v_ref[pl.ds(hd, tm, stride=heads), :] = v[:, sl]
    if transposed:
        kb_ref, qt_ref, vt_ref = rest
        kb_ref[...] = k.astype(BF16)
        for hd in range(heads):
            sl = slice(hd * LANES, (hd + 1) * LANES)
            qt_ref[0, hd] = q[:, sl].T.astype(BF16)
            vt_ref[0, hd] = v[:, sl].T.astype(BF16)
    else:
        (q_ref,) = rest
        q_ref[...] = q.astype(BF16)


def _pre_even(x2d, g, w_in, layer, e, tm, seq_len, transposed):
    n, d = x2d.shape
    pw = d // 2
    dw = d // 2
    heads = dw // LANES
    row = lambda c: pl.BlockSpec((tm, c), lambda i: (i, 0))
    kv_rows = pl.BlockSpec((tm * heads, LANES), lambda i: (i, 0))
    out_specs = [row(pw), kv_rows, kv_rows]
    out_shape = [jax.ShapeDtypeStruct((n, pw), F32), jax.ShapeDtypeStruct((n * heads, LANES), F32),
                 jax.ShapeDtypeStruct((n * heads, LANES), F32)]
    if transposed:
        tps = seq_len // tm
        tspec = pl.BlockSpec((1, heads, LANES, tm), lambda i: (i // tps, 0, 0, i % tps))
        tshape = jax.ShapeDtypeStruct((n // seq_len, heads, LANES, seq_len), BF16)
        out_specs += [row(dw), tspec, tspec]
        out_shape += [jax.ShapeDtypeStruct((n, dw), BF16), tshape, tshape]
    else:
        out_specs += [row(dw)]
        out_shape += [jax.ShapeDtypeStruct((n, dw), BF16)]
    return pl.pallas_call(
        functools.partial(_pre_even_kernel, pw=pw, dw=dw, transposed=transposed),
        grid=(n // tm,),
        in_specs=[row(d), _layer_spec(g, layer), _layer_spec(w_in, e)],
        out_specs=out_specs,
        out_shape=out_shape,
        compiler_params=_cparams(("parallel",)),
        name="pre_even",
    )(x2d, g, w_in)


def _pool_kernel(u_ref, uprev_ref, hist_ref, w_ref, sc_ref, o_ref, *, tp, pos0, gw):
    t = pl.program_id(1)
    u = u_ref[0]
    prev = jnp.where(t == 0, hist_ref[0], uprev_ref[0])
    full = jnp.concatenate([prev, u], axis=0)
    sums = []
    s = full
    for sh in (1, 2, 4, 8):
        s = s + pltpu.roll(s, sh, 0)
        sums.append(s)
    pos = pos0 + t * tp + lax.broadcasted_iota(jnp.int32, (tp, 1), 0)
    for g, w in enumerate(POOL_WINDOWS):
        sl = slice(g * gw, (g + 1) * gw)
        cnt = jnp.minimum(pos + 1, w).astype(F32)
        pooled = sums[g][POOL_HALO:, sl] / cnt - u[:, sl]
        y = _bdot(pooled, w_ref[g]) * sc_ref[:, sl]
        o_ref[0, :, sl] = y.astype(BF16)


def _pool(u3, hist16, pool_w, scale, e, tp, pos0):
    b, t, pw = u3.shape
    gw = pw // POOL_GROUPS
    hb = tp // POOL_HALO
    return pl.pallas_call(
        functools.partial(_pool_kernel, tp=tp, pos0=pos0, gw=gw),
        grid=(b, t // tp),
        in_specs=[pl.BlockSpec((1, tp, pw), lambda i, j: (i, j, 0)),
                  pl.BlockSpec((1, POOL_HALO, pw), lambda i, j: (i, jnp.maximum(j * hb - 1, 0), 0)),
                  pl.BlockSpec((1, POOL_HALO, pw), lambda i, j: (i, 0, 0)),
                  _layer_spec(pool_w, e), _layer_spec(scale, e)],
        out_specs=pl.BlockSpec((1, tp, pw), lambda i, j: (i, j, 0)),
        out_shape=jax.ShapeDtypeStruct((b, t, pw), BF16),
        compiler_params=_cparams(("parallel", "parallel")),
        name="pool_mixer",
    )(u3, u3, hist16, pool_w, scale)


def _lam(lq1, lk1, lq2, lk2, lam_init):
    return (jnp.exp(jnp.sum(lq1[...] * lk1[...], axis=-1, keepdims=True))
            - jnp.exp(jnp.sum(lq2[...] * lk2[...], axis=-1, keepdims=True)) + lam_init)


def _stack_maps(q):
    lane = lax.broadcasted_iota(jnp.int32, q.shape, 1)
    zero = jnp.zeros_like(q)
    return jnp.concatenate([jnp.where(lane < DIFF_DH, q, zero), jnp.where(lane >= DIFF_DH, q, zero)], axis=0)


def _softmax_step(s, v, m_prev, l_prev, acc_prev):
    m_new = jnp.maximum(m_prev, jnp.max(s, axis=-1, keepdims=True))
    alpha = jnp.exp(m_prev - m_new)
    p = jnp.exp(s - m_new)
    l_new = alpha * l_prev + jnp.sum(p, axis=-1, keepdims=True)
    acc_new = alpha * acc_prev + _bdot(p, v)
    return m_new, l_new, acc_new


def _diff_finish(l, acc, lam, subg, rows, out_scale):
    o = acc[:rows] / l[:rows] - lam * (acc[rows:] / l[rows:])
    o = o * lax.rsqrt(jnp.mean(o * o, axis=-1, keepdims=True) + NORM_EPS) * subg
    return o * out_scale


def _diffp_kernel(qi_ref, kj_ref, qt_ref, k_ref, vt_ref, lq1, lk1, lq2, lk2, subg_ref, o_ref,
                  q2_ref, m_ref, l_ref, acc_ref, *, tq, hps, lam_init):
    step_id = pl.program_id(2)
    i = qi_ref[step_id]
    j = kj_ref[step_id]

    @pl.when(j == 0)
    def _():
        for hh in range(hps):
            qt = qt_ref[0, hh]
            row = lax.broadcasted_iota(jnp.int32, qt.shape, 0)
            zero = jnp.zeros_like(qt)
            q2_ref[hh, :, :tq] = jnp.where(row < DIFF_DH, qt, zero)
            q2_ref[hh, :, tq:] = jnp.where(row >= DIFF_DH, qt, zero)
        m_ref[...] = jnp.full(m_ref.shape, NEG_INF, F32)
        l_ref[...] = jnp.zeros(l_ref.shape, F32)
        acc_ref[...] = jnp.zeros(acc_ref.shape, F32)

    def step(masked):
        for hh in range(hps):
            s = jnp.dot(k_ref[0, :, hh * LANES:(hh + 1) * LANES], q2_ref[hh],
                        preferred_element_type=F32)
            if masked:
                kc = _div_pow2(lax.broadcasted_iota(jnp.int32, s.shape, 0), CHUNK)
                qc = _div_pow2(_mod_pow2(lax.broadcasted_iota(jnp.int32, s.shape, 1), tq), CHUNK)
                s = jnp.where(kc <= qc, s, NEG_INF)
            m_prev = m_ref[hh]
            m_new = jnp.maximum(m_prev, jnp.max(s, axis=0, keepdims=True))
            alpha = jnp.exp(m_prev - m_new)
            p = jnp.exp(s - m_new)
            l_ref[hh] = alpha * l_ref[hh] + jnp.sum(p, axis=0, keepdims=True)
            acc_ref[hh] = alpha * acc_ref[hh] + jnp.dot(vt_ref[0, hh], p.astype(BF16),
                                                        preferred_element_type=F32)
            m_ref[hh] = m_new

    @pl.when(j < i)
    def _():
        step(False)

    @pl.when(j == i)
    def _():
        step(True)
        lam = _lam(lq1, lk1, lq2, lk2, lam_init)
        for hh in range(hps):
            acc = acc_ref[hh]
            l = l_ref[hh]
            ot = acc[:, :tq] / l[:, :tq] - lam * (acc[:, tq:] / l[:, tq:])
            ot = ot * lax.rsqrt(jnp.mean(ot * ot, axis=0, keepdims=True) + NORM_EPS)
            o_ref[0, :, hh * LANES:(hh + 1) * LANES] = (ot.T * subg_ref[...] * (1.0 - lam_init)).astype(BF16)


def _diff_prompt(qt, kb3, vt, lam_vecs, subg, e, lam_init, tq, hps):
    b, t, w = kb3.shape
    heads = w // LANES
    nq = t // tq
    pairs = [(i, j) for i in range(nq) for j in range(i + 1)]
    qi = jnp.asarray(np.array([p[0] for p in pairs], np.int32))
    kj = jnp.asarray(np.array([p[1] for p in pairs], np.int32))
    qspec = pl.BlockSpec((1, hps, LANES, tq), lambda bb, h, s, qi_r, kj_r: (bb, h, 0, qi_r[s]))
    kspec = pl.BlockSpec((1, tq, hps * LANES), lambda bb, h, s, qi_r, kj_r: (bb, kj_r[s], h))
    vspec = pl.BlockSpec((1, hps, LANES, tq), lambda bb, h, s, qi_r, kj_r: (bb, h, 0, kj_r[s]))
    vec = lambda a: pl.BlockSpec((None,) + tuple(a.shape[1:]), lambda *_: (e, 0, 0))
    grid_spec = pltpu.PrefetchScalarGridSpec(
        num_scalar_prefetch=2,
        grid=(b, heads // hps, len(pairs)),
        in_specs=[qspec, kspec, vspec] + [vec(a) for a in lam_vecs] + [vec(subg)],
        out_specs=pl.BlockSpec((1, tq, hps * LANES), lambda bb, h, s, qi_r, kj_r: (bb, qi_r[s], h)),
        scratch_shapes=[pltpu.VMEM((hps, LANES, 2 * tq), BF16), pltpu.VMEM((hps, 1, 2 * tq), F32),
                        pltpu.VMEM((hps, 1, 2 * tq), F32), pltpu.VMEM((hps, LANES, 2 * tq), F32)],
    )
    return pl.pallas_call(
        functools.partial(_diffp_kernel, tq=tq, hps=hps, lam_init=lam_init),
        grid_spec=grid_spec,
        out_shape=jax.ShapeDtypeStruct((b, t, w), BF16),
        compiler_params=_cparams(("parallel", "parallel", "arbitrary")),
        name="diff_attn_prompt",
    )(qi, kj, qt, kb3, vt, *lam_vecs, subg)


def _diffs_kernel(q_ref, ck_ref, cv_ref, kn_ref, vn_ref, lq1, lk1, lq2, lk2, subg_ref, o_ref,
                  m_ref, l_ref, acc_ref, *, ts, heads, lam_init):
    j = pl.program_id(1)
    nj = pl.num_programs(1)
    nq = heads * 2 * ts
    q2 = jnp.concatenate([_stack_maps(q_ref[0, :, h * LANES:(h + 1) * LANES]) for h in range(heads)], axis=0)

    def visible(ncols, valid_cols):
        qh = _div_pow2(lax.broadcasted_iota(jnp.int32, (nq, ncols), 0), 2 * ts)
        col = lax.broadcasted_iota(jnp.int32, (nq, ncols), 1)
        ok = _mod_pow2(col, heads) == qh
        return ok if valid_cols is None else ok & (col < valid_cols)

    @pl.when(j == 0)
    def _():
        nnew = kn_ref.shape[1]
        pad = jnp.zeros((LANES - nnew, LANES), F32)
        kn = jnp.concatenate([kn_ref[0], pad], axis=0)
        vn = jnp.concatenate([vn_ref[0], pad], axis=0)
        s = jnp.where(visible(LANES, nnew), _bdot_nt(q2, kn), NEG_INF)
        m, l, acc = _softmax_step(s, vn, jnp.full((nq, 1), NEG_INF, F32), jnp.zeros((nq, 1), F32),
                                  jnp.zeros((nq, LANES), F32))
        m_ref[...] = m
        l_ref[...] = l
        acc_ref[...] = acc

    s = _bdot_nt(q2, ck_ref[...])
    s = jnp.where(visible(s.shape[1], None), s, NEG_INF)
    m, l, acc = _softmax_step(s, cv_ref[...], m_ref[...], l_ref[...], acc_ref[...])
    m_ref[...] = m
    l_ref[...] = l
    acc_ref[...] = acc

    @pl.when(j == nj - 1)
    def _():
        lam = _lam(lq1, lk1, lq2, lk2, lam_init)
        l_all = l_ref[...]
        acc_all = acc_ref[...]
        for h in range(heads):
            rows = slice(h * 2 * ts, (h + 1) * 2 * ts)
            o_ref[0, :, h * LANES:(h + 1) * LANES] = _diff_finish(
                l_all[rows], acc_all[rows], lam, subg_ref[...], ts, 1.0 - lam_init).astype(BF16)


def _diff_sample(q3, ck5, cv5, kn_rows, vn_rows, lam_vecs, subg, e, lam_init, tk):
    b, ts, w = q3.shape
    nl, _, past, heads, _ = ck5.shape
    assert heads * LANES == w and heads * ts <= LANES
    assert past % CHUNK == 0 and ts <= CHUNK and past % tk == 0
    rows = lambda a: a.reshape(a.shape[:-3] + (a.shape[-3] * heads, LANES))
    qspec = pl.BlockSpec((1, ts, w), lambda bb, j: (bb, 0, 0))
    new = pl.BlockSpec((1, ts * heads, LANES), lambda bb, j: (bb, 0, 0))
    cache = pl.BlockSpec((None, None, tk * heads, LANES), lambda bb, j: (e, bb, j, 0))
    vec = lambda a: pl.BlockSpec((None,) + tuple(a.shape[1:]), lambda *_: (e, 0, 0))
    nq = heads * 2 * ts
    return pl.pallas_call(
        functools.partial(_diffs_kernel, ts=ts, heads=heads, lam_init=lam_init),
        grid=(b, past // tk),
        in_specs=[qspec, cache, cache, new, new] + [vec(a) for a in lam_vecs] + [vec(subg)],
        out_specs=qspec,
        out_shape=jax.ShapeDtypeStruct((b, ts, w), BF16),
        scratch_shapes=[pltpu.VMEM((nq, 1), F32), pltpu.VMEM((nq, 1), F32), pltpu.VMEM((nq, LANES), F32)],
        compiler_params=_cparams(("parallel", "arbitrary")),
        name="diff_attn_sample",
    )(q3, rows(ck5), rows(cv5), kn_rows, vn_rows, *lam_vecs, subg)


def _mixxa_kernel(x_ref, m1_ref, m2_ref, w1_ref, w2_ref, g_ref, wq_ref, mk_ref, mv_ref, wo_ref, o_ref,
                  *, nseq, heads):
    x1 = x_ref[...] + _bdot(m1_ref[...], w1_ref[...]) + _bdot(m2_ref[...], w2_ref[...])
    tm, d = x1.shape
    dh = d // heads
    h = _rms(x1, g_ref[...])
    q = (_bdot(h, wq_ref[...]) * (dh ** -0.5)).astype(BF16)
    q3 = q.reshape(nseq, tm // nseq, d)
    outs = []
    for hd in range(heads):
        sl = slice(hd * dh, (hd + 1) * dh)
        s = jnp.einsum("bqd,bkd->bqk", q3[:, :, sl], mk_ref[:, :, sl], preferred_element_type=F32)
        p = jnp.exp(s - jnp.max(s, axis=-1, keepdims=True))
        p = p / jnp.sum(p, axis=-1, keepdims=True)
        outs.append(jnp.einsum("bqk,bkd->bqd", p.astype(BF16), mv_ref[:, :, sl], preferred_element_type=F32))
    o = jnp.concatenate(outs, axis=-1).reshape(tm, d)
    o_ref[...] = x1 + _bdot(o, wo_ref[...])


def _mix_xa(x2d, m1, m2, w_mix, lmix, g, wq, mk, mv, wo, layer, tm, rows_per_seq):
    n, d = x2d.shape
    half = d // 2
    nseq = max(tm // rows_per_seq, 1)
    tiles_per_seq = max(rows_per_seq // tm, 1)
    row = lambda c: pl.BlockSpec((tm, c), lambda i: (i, 0))
    col = lambda c: pl.BlockSpec((tm, half), lambda i: (i, c))
    mem = pl.BlockSpec((None, nseq) + mk.shape[2:], lambda i: (layer, i // tiles_per_seq, 0, 0))
    return pl.pallas_call(
        functools.partial(_mixxa_kernel, nseq=nseq, heads=XA_HEADS),
        grid=(n // tm,),
        in_specs=[row(d), col(m1[1]), col(m2[1]),
                  _layer_spec(w_mix, lmix, (half, d), (0, 0)), _layer_spec(w_mix, lmix, (half, d), (1, 0)),
                  _layer_spec(g, layer), _layer_spec(wq, layer), mem, mem, _layer_spec(wo, layer)],
        out_specs=row(d),
        out_shape=jax.ShapeDtypeStruct((n, d), F32),
        compiler_params=_cparams(("parallel",)),
        name="mix_xa",
    )(x2d, m1[0], m2[0], w_mix, w_mix, g, wq, mk, mv, wo)


def _ffn_kernel(x_ref, g_ref, wg_ref, wu_ref, wd_ref, gf_ref, o_ref, *, nchunk, final):
    x = x_ref[...]
    h = _rms(x, g_ref[...]).astype(BF16)
    ff = wg_ref.shape[1]
    fc = ff // nchunk
    acc = x
    for c in range(nchunk):
        sl = slice(c * fc, (c + 1) * fc)
        a = jnp.dot(h, wg_ref[:, sl], preferred_element_type=F32)
        b = jnp.dot(h, wu_ref[:, sl], preferred_element_type=F32)
        acc = acc + _bdot(a * _sigmoid(a) * b, wd_ref[sl, :])
    if final:
        acc = _rms(acc, gf_ref[...])
    o_ref[...] = acc


def _ffn(x2d, g, wg, wu, wd, gf, layer, tm, final):
    n, d = x2d.shape
    row = pl.BlockSpec((tm, d), lambda i: (i, 0))
    return pl.pallas_call(
        functools.partial(_ffn_kernel, nchunk=2, final=final),
        grid=(n // tm,),
        in_specs=[row, _layer_spec(g, layer), _layer_spec(wg, layer), _layer_spec(wu, layer),
                  _layer_spec(wd, layer), pl.BlockSpec((1, d), lambda i: (0, 0))],
        out_specs=row,
        out_shape=jax.ShapeDtypeStruct((n, d), F32),
        compiler_params=_cparams(("parallel",)),
        name="ffn",
    )(x2d, g, wg, wu, wd, gf)


_RW_PARAMS = ("mu", "wr", "wk", "wv", "w0", "w1", "w2", "a0", "a1", "a2", "g1", "g2", "k_k", "k_a")


def _pre_rwkv_kernel(*refs, tm, rows_per_seq, hl_rows, has_vres):
    (x_ref, xp_ref, first_ref, g_ref, mu_ref, wr_ref, wk_ref, wv_ref, w0_ref, w1_ref, w2_ref,
     a0_ref, a1_ref, a2_ref, g1_ref, g2_ref, kk_ref, ka_ref) = refs[:18]
    pos = 18
    if has_vres:
        vf_ref, v0_ref, v1_ref, v2_ref = refs[pos:pos + 4]
        pos += 4
    r_ref, k_ref, v_ref, kr_ref, ag_ref, w_ref, gate_ref, hl_ref = refs[pos:]

    i = pl.program_id(0)
    g = g_ref[...]
    h = _rms(x_ref[...], g)
    hp_last = _rms(xp_ref[...], g)[xp_ref.shape[0] - 1:, :]
    rowl = lax.broadcasted_iota(jnp.int32, (tm, 1), 0)
    shifted = jnp.where(rowl == 0, hp_last, pltpu.roll(h, 1, 0))
    is_start = _mod_pow2(i * tm + rowl, rows_per_seq) == 0
    h_prev = jnp.where(is_start, first_ref[...], shifted)
    xx = h_prev - h
    mix = lambda n: (h + xx * mu_ref[n:n + 1, :]).astype(BF16)
    xr, xw, xk, xv, xa, xg = (mix(n) for n in range(6))

    r_ref[...] = _bdot(xr, wr_ref[...]).astype(r_ref.dtype)
    z = w0_ref[...] + _bdot(jnp.tanh(_bdot(xw, w1_ref[...])), w2_ref[...])
    w_ref[...] = -(jnp.maximum(-z, 0.0) + jnp.log1p(jnp.exp(-jnp.abs(z)))) - 0.5
    k = _bdot(xk, wk_ref[...])
    v = _bdot(xv, wv_ref[...])
    if has_vres:
        v = v + (vf_ref[...].astype(F32) - v) * _sigmoid(v0_ref[...] + _bdot(_bdot(xv, v1_ref[...]), v2_ref[...]))
    v_ref[...] = v.astype(v_ref.dtype)
    a = _sigmoid(a0_ref[...] + _bdot(_bdot(xa, a1_ref[...]), a2_ref[...]))
    ag_ref[...] = a.astype(ag_ref.dtype)
    gate_ref[...] = _bdot(_sigmoid(_bdot(xg, g1_ref[...])), g2_ref[...]).astype(gate_ref.dtype)
    kr_ref[...] = (k * kk_ref[...]).astype(kr_ref.dtype)
    k_ref[...] = (k * (1.0 + (a - 1.0) * ka_ref[...])).astype(k_ref.dtype)
    hl_ref[...] = h[tm - hl_rows:, :]


def _pre_rwkv(x2d, first, g, W, layer, o, vf, tm, rows_per_seq, hl_rows):
    n, d = x2d.shape
    has_vres = vf is not None
    xp_rows = 8
    row = pl.BlockSpec((tm, d), lambda i: (i, 0))
    ins = [x2d, x2d, first, g] + [W["rw_" + nm] for nm in _RW_PARAMS]
    specs = [row,
             pl.BlockSpec((xp_rows, d), lambda i: (jnp.maximum(i * (tm // xp_rows) - 1, 0), 0)),
             pl.BlockSpec((tm, d), lambda i: (i // max(rows_per_seq // tm, 1), 0)),
             _layer_spec(g, layer)] + [_layer_spec(W["rw_" + nm], o) for nm in _RW_PARAMS]
    if has_vres:
        vres = [W["rw_v0"], W["rw_v1"], W["rw_v2"]]
        ins += [vf] + vres
        specs += [row] + [_layer_spec(a, o - 1) for a in vres]
    nt = n // tm
    dts = [BF16, BF16, BF16, BF16, BF16, F32, BF16]
    return pl.pallas_call(
        functools.partial(_pre_rwkv_kernel, tm=tm, rows_per_seq=rows_per_seq, hl_rows=hl_rows,
                          has_vres=has_vres),
        grid=(nt,),
        in_specs=specs,
        out_specs=[row] * 7 + [pl.BlockSpec((hl_rows, d), lambda i: (i, 0))],
        out_shape=[jax.ShapeDtypeStruct((n, d), dt) for dt in dts]
        + [jax.ShapeDtypeStruct((nt * hl_rows, d), F32)],
        compiler_params=_cparams(("parallel",)),
        name="pre_rwkv",
    )(*ins)


def _split3(x):
    x1 = x.astype(BF16)
    r1 = x - x1.astype(F32)
    x2 = r1.astype(BF16)
    x3 = (r1 - x2.astype(F32)).astype(BF16)
    return x1, x2, x3


def _head_sum(x, lo_mask):
    s_lo = jnp.sum(jnp.where(lo_mask, x, 0.0), axis=-1, keepdims=True)
    s_hi = jnp.sum(jnp.where(lo_mask, 0.0, x), axis=-1, keepdims=True)
    return jnp.where(lo_mask, s_lo, s_hi)


def _rwkv_rec_kernel(r_ref, k_ref, v_ref, kr_ref, ag_ref, w_ref, gate_ref, lng_ref, lnb_ref, rk_ref, s0_ref,
                     y_ref, sfin_ref, s_scr, *, L, pairs):
    c = pl.program_id(2)
    nc = pl.num_programs(2)
    P = range(pairs)

    @pl.when(c == 0)
    def _():
        zero = jnp.zeros((RW_N, RW_N), F32)
        for p in P:
            top = jnp.concatenate([s0_ref[0, 2 * p], zero], axis=1)
            bot = jnp.concatenate([zero, s0_ref[0, 2 * p + 1]], axis=1)
            s_scr[p] = jnp.concatenate([top, bot], axis=0)

    L2 = 2 * L
    sub = min(16, L)
    nblk = L // sub
    lane = lax.broadcasted_iota(jnp.int32, (L, LANES), 1)
    lo = lane < RW_N
    row2 = lax.broadcasted_iota(jnp.int32, (L2, L2), 0)
    col2 = lax.broadcasted_iota(jnp.int32, (L2, L2), 1)
    same = _div_pow2(row2, L) == _div_pow2(col2, L)
    strict = same & (col2 < row2)
    incl = same & (col2 <= row2)
    diag_blk = _div_pow2(row2, sub) == _div_pow2(col2, sub)
    eye = (row2 == col2).astype(F32)
    tri = (lax.broadcasted_iota(jnp.int32, (L, L), 1) <= lax.broadcasted_iota(jnp.int32, (L, L), 0)).astype(BF16)
    dot = lambda a, b: jnp.dot(a, b, preferred_element_type=F32)

    def stack(x):
        xb = x.astype(BF16)
        zero = jnp.zeros_like(xb)
        return jnp.concatenate([jnp.where(lo, xb, zero), jnp.where(lo, zero, xb)], axis=0)

    sl = [slice(p * LANES, (p + 1) * LANES) for p in P]
    r = [r_ref[0, :, sl[p]].astype(F32) for p in P]
    k = [k_ref[0, :, sl[p]].astype(F32) for p in P]
    v = [v_ref[0, :, sl[p]].astype(F32) for p in P]
    logd = [-jnp.exp(w_ref[0, :, sl[p]]) for p in P]
    kk = []
    for p in P:
        kr = kr_ref[0, :, sl[p]].astype(F32)
        kk.append(kr / jnp.maximum(jnp.sqrt(_head_sum(kr * kr, lo)), 1e-12))
    cum = []
    for p in P:
        l1, l2, l3 = _split3(logd[p])
        cum.append(dot(tri, l1) + dot(tri, l2) + dot(tri, l3))
    ar_s, bk_s, v_s, bkh_s, p_last = [], [], [], [], []
    for p in P:
        e_pos = jnp.exp(cum[p])
        e_neg = jnp.exp(-cum[p])
        pl_ = e_pos[L - 1:, :]
        bt = kk[p] * ag_ref[0, :, sl[p]].astype(F32) * e_neg
        kt = k[p] * e_neg
        ar_s.append(jnp.concatenate([stack(-kk[p] * jnp.exp(cum[p] - logd[p])), stack(r[p] * e_pos)], axis=0))
        bk_s.append(jnp.concatenate([stack(bt), stack(kt)], axis=0))
        bkh_s.append(jnp.concatenate([stack(bt * pl_), stack(kt * pl_)], axis=0))
        v_s.append(stack(v[p]))
        p_last.append(pl_)
    s_bd = [s_scr[p] for p in P]
    ss = [_bdot_nt(ar_s[p], s_bd[p]) for p in P]
    g_all = [_bdot_nt(ar_s[p], bk_s[p]) for p in P]
    a_ab = [jnp.where(strict, g_all[p][:L2, :L2], 0.0) for p in P]
    a_akrk = [jnp.concatenate([jnp.where(strict, g_all[p][:L2, L2:], 0.0),
                               jnp.where(incl, g_all[p][L2:, L2:], 0.0)], axis=0) for p in P]
    a_rb = [jnp.where(incl, g_all[p][L2:, :L2], 0.0) for p in P]
    sv = [_bdot(a_akrk[p], v_s[p]) for p in P]
    wmat = [ss[p][:L2] + sv[p][:L2] for p in P]

    a_d = [jnp.where(diag_blk, a_ab[p], 0.0) for p in P]
    pm = [eye + a_d[p] for p in P]
    pw = a_d
    for _ in range(_log2(sub) - 1):
        pw = [_bdot(pw[p], pw[p]) for p in P]
        pm = [pm[p] + _bdot(pw[p], pm[p]) for p in P]
    if nblk > 1:
        xn = [_bdot(pm[p], jnp.concatenate([wmat[p], a_ab[p] - a_d[p]], axis=1)) for p in P]
        u = [xn[p][:, :LANES] for p in P]
        nm = [xn[p][:, LANES:] for p in P]
        steps = _log2(nblk)
        for it in range(steps):
            u = [u[p] + _bdot(nm[p], u[p]) for p in P]
            if it < steps - 1:
                nm = [_bdot(nm[p], nm[p]) for p in P]
    else:
        u = [_bdot(pm[p], wmat[p]) for p in P]

    y2 = [ss[p][L2:] + sv[p][L2:] + _bdot(a_rb[p], u[p]) for p in P]
    for p in P:
        uv = jnp.concatenate([u[p].astype(BF16), v_s[p]], axis=0)
        s_scr[p] = s_bd[p] * p_last[p] + _bdot_tn(uv, bkh_s[p])
    for p in P:
        y = y2[p][:L] + y2[p][L:]
        mean = _head_sum(y, lo) * (1.0 / RW_N)
        yc = y - mean
        var = _head_sum(yc * yc, lo) * (1.0 / RW_N)
        yn = yc * lax.rsqrt(var + RW_LN_EPS) * lng_ref[:, sl[p]] + lnb_ref[:, sl[p]]
        bonus = _head_sum(r[p] * k[p] * rk_ref[:, sl[p]], lo) * v[p]
        y_ref[0, :, sl[p]] = ((yn + bonus) * gate_ref[0, :, sl[p]].astype(F32)).astype(BF16)

    @pl.when(c == nc - 1)
    def _():
        for p in P:
            s = s_scr[p]
            sfin_ref[0, 2 * p] = s[:RW_N, :RW_N]
            sfin_ref[0, 2 * p + 1] = s[RW_N:, RW_N:]


def _rwkv_rec(arrs, lng, lnb, rk, s0, o, L, pairs):
    b, t, d = arrs[0].shape
    heads = d // RW_N
    groups = heads // (2 * pairs)
    w = pairs * LANES
    tok = pl.BlockSpec((1, L, w), lambda bb, gg, c: (bb, c, gg))
    vec = pl.BlockSpec((None, 1, w), lambda bb, gg, c: (o, 0, gg))
    st_in = pl.BlockSpec((None, 1, 2 * pairs, RW_N, RW_N), lambda bb, gg, c: (o, bb, gg, 0, 0))
    st_out = pl.BlockSpec((1, 2 * pairs, RW_N, RW_N), lambda bb, gg, c: (bb, gg, 0, 0))
    return pl.pallas_call(
        functools.partial(_rwkv_rec_kernel, L=L, pairs=pairs),
        grid=(b, groups, t // L),
        in_specs=[tok] * 7 + [vec, vec, vec, st_in],
        out_specs=[tok, st_out],
        out_shape=[jax.ShapeDtypeStruct((b, t, d), BF16),
                   jax.ShapeDtypeStruct((b, heads, RW_N, RW_N), F32)],
        scratch_shapes=[pltpu.VMEM((pairs, LANES, LANES), F32)],
        compiler_params=_cparams(("parallel", "parallel", "arbitrary")),
        name="rwkv_rec",
    )(*arrs, lng, lnb, rk, s0)


def _trunk(x, pos0, diff_k_past, diff_v_past, pool_hist, rw_shift, rw_state, mem_k, mem_v, W, cfg):
    b, t, d = x.shape
    n = b * t
    depth = W["norm_mix_g"].shape[0]
    tm, tm_rw, tm_xa = cfg["tm"], cfg["tm_rw"], cfg["tm_xa"]
    x2 = x.reshape(n, d)
    new_k, new_v, new_pool, new_shift, new_s = [], [], [], [], []
    v_first = None
    half = d // 2
    heads = half // LANES
    for l in range(depth):
        if l % 2 == 0:
            e = l // 2
            prompt = diff_k_past is None
            outs = _pre_even(x2, W["norm_mix_g"], W["ev_w_in"], l, e, tm, t, transposed=prompt)
            u, k_rows, v_rows = outs[:3]
            new_k.append(k_rows.reshape(b, t, heads, 2 * DIFF_DH))
            new_v.append(v_rows.reshape(b, t, heads, DIFF_VD))
            u3 = u.reshape(b, t, half)
            new_pool.append(u3[:, t - POOL_HIST:])
            hist16 = jnp.pad(pool_hist[e], ((0, 0), (POOL_HALO - POOL_HIST, 0), (0, 0)))
            m1 = _pool(u3, hist16, W["ev_pool_w"], W["ev_pool_scale"], e, cfg["tp"], pos0).reshape(n, half)
            lam_init = 0.8 - 0.6 * math.exp(-0.3 * l)
            lam_vecs = [W[nm] for nm in ("ev_lam_q1", "ev_lam_k1", "ev_lam_q2", "ev_lam_k2")]
            if prompt:
                kb, qt, vt = outs[3:]
                m2 = _diff_prompt(qt, kb.reshape(b, t, half), vt, lam_vecs, W["ev_subln_g"], e, lam_init,
                                  cfg["tq"], cfg["hps"])
            else:
                m2 = _diff_sample(outs[3].reshape(b, t, half), diff_k_past, diff_v_past,
                                  k_rows.reshape(b, t * heads, LANES), v_rows.reshape(b, t * heads, LANES),
                                  lam_vecs, W["ev_subln_g"], e, lam_init, cfg["tk"])
            m1, m2 = (m1, 0), (m2.reshape(n, half), 0)
            w_mix, lmix = W["ev_w_out"], e
        else:
            o = l // 2
            rows_first = jnp.broadcast_to(rw_shift[o][:, None, :], (b, min(t, tm_rw), d)).reshape(-1, d)
            outs = _pre_rwkv(x2, rows_first, W["norm_mix_g"], W, l, o, None if o == 0 else v_first, tm_rw, t,
                             cfg["hl_rows"])
            r, k, v, kr, ag, wl, gate, hl = outs
            if o == 0:
                v_first = v
            if cfg["hl_rows"] == tm_rw:
                new_shift.append(hl.reshape(b, t, d)[:, -1])
            else:
                per_seq = t // tm_rw
                new_shift.append(hl.reshape(b, per_seq, cfg["hl_rows"], d)[:, -1, -1])
            arrs = [a.reshape(b, t, d) for a in (r, k, v, kr, ag, wl, gate)]
            y, s_fin = _rwkv_rec(arrs, W["rw_lnx_g"], W["rw_lnx_b"], W["rw_r_k"], rw_state, o, cfg["L"],
                                 cfg["pairs"])
            new_s.append(s_fin)
            y2 = y.reshape(n, d)
            m1, m2 = (y2, 0), (y2, 1)
            w_mix, lmix = W["rw_wo"], o
        x2 = _mix_xa(x2, m1, m2, w_mix, lmix, W["norm_xa_g"], W["xa_wq"], mem_k, mem_v, W["xa_wo"], l, tm_xa, t)
        x2 = _ffn(x2, W["norm_ffn_g"], W["ffn_wg"], W["ffn_wu"], W["ffn_wd"], W["final_norm_g"], l, tm,
                  final=(l == depth - 1))
    return (x2.reshape(b, t, d), jnp.stack(new_k), jnp.stack(new_v), jnp.stack(new_pool),
            jnp.stack(new_shift), jnp.stack(new_s))


def kernel(x_prompt, x_sample, cache_diff_k, cache_diff_v, state_pool, state_rw_shift, state_rw_wkv, cache_mem_k, cache_mem_v, mem_prompt, norm_mix_g, norm_xa_g, norm_ffn_g, final_norm_g, ev_w_in, ev_pool_w, ev_pool_scale, ev_lam_q1, ev_lam_k1, ev_lam_q2, ev_lam_k2, ev_subln_g, ev_w_out, rw_mu, rw_wr, rw_wk, rw_wv, rw_wo, rw_w0, rw_w1, rw_w2, rw_a0, rw_a1, rw_a2, rw_v0, rw_v1, rw_v2, rw_g1, rw_g2, rw_k_k, rw_k_a, rw_r_k, rw_lnx_g, rw_lnx_b, xa_wq, xa_wk, xa_wv, xa_wo, ffn_wg, ffn_wu, ffn_wd):
    bf = lambda a: a.astype(BF16)
    vec = lambda a: a.reshape(a.shape[0], 1, -1)
    W = dict(
        norm_mix_g=vec(norm_mix_g), norm_xa_g=vec(norm_xa_g), norm_ffn_g=vec(norm_ffn_g),
        final_norm_g=final_norm_g.reshape(1, -1),
        ev_w_in=bf(ev_w_in), ev_pool_w=bf(ev_pool_w), ev_pool_scale=vec(ev_pool_scale),
        ev_lam_q1=vec(ev_lam_q1), ev_lam_k1=vec(ev_lam_k1), ev_lam_q2=vec(ev_lam_q2), ev_lam_k2=vec(ev_lam_k2),
        ev_subln_g=vec(ev_subln_g), ev_w_out=bf(ev_w_out),
        rw_mu=rw_mu, rw_wr=bf(rw_wr), rw_wk=bf(rw_wk), rw_wv=bf(rw_wv), rw_wo=bf(rw_wo),
        rw_w0=vec(rw_w0), rw_w1=bf(rw_w1), rw_w2=bf(rw_w2), rw_a0=vec(rw_a0), rw_a1=bf(rw_a1), rw_a2=bf(rw_a2),
        rw_v0=vec(rw_v0), rw_v1=bf(rw_v1), rw_v2=bf(rw_v2), rw_g1=bf(rw_g1), rw_g2=bf(rw_g2),
        rw_k_k=vec(rw_k_k), rw_k_a=vec(rw_k_a), rw_r_k=vec(rw_r_k),
        rw_lnx_g=vec(rw_lnx_g), rw_lnx_b=vec(rw_lnx_b),
        xa_wq=bf(xa_wq), xa_wo=bf(xa_wo), ffn_wg=bf(ffn_wg), ffn_wu=bf(ffn_wu), ffn_wd=bf(ffn_wd),
    )
    depth, d = norm_mix_g.shape
    bp, tp_len = x_prompt.shape[:2]
    bs, ts = x_sample.shape[:2]
    n_even = ev_w_in.shape[0]
    n_odd = rw_wr.shape[0]
    nmem = mem_prompt.shape[1]

    kv_f32, kv_bf = _mem_kv(mem_prompt.reshape(bp * nmem, d), bf(jnp.concatenate([xa_wk, xa_wv], axis=0)))
    p_mem_k = kv_f32[:depth].reshape(depth, bp, nmem, XA_HEADS, d // XA_HEADS)
    p_mem_v = kv_f32[depth:].reshape(depth, bp, nmem, XA_HEADS, d // XA_HEADS)
    pmk = kv_bf[:depth].reshape(depth, bp, nmem, d)
    pmv = kv_bf[depth:].reshape(depth, bp, nmem, d)

    cfg_p = dict(tm=512, tm_rw=256, tm_xa=512, tp=512, tq=512, hps=2, tk=None, L=64, pairs=8, hl_rows=8)
    zero_pool = jnp.zeros((n_even, bp, POOL_HIST, d // 2), F32)
    zero_shift = jnp.zeros((n_odd, bp, d), F32)
    zero_wkv = jnp.zeros((n_odd, bp, d // RW_N, RW_N, RW_N), F32)
    outs_p = _trunk(x_prompt, 0, None, None, zero_pool, zero_shift, zero_wkv, pmk, pmv, W, cfg_p)

    past = cache_diff_k.shape[2]
    cfg_s = dict(tm=bs * ts, tm_rw=bs * ts, tm_xa=8 * ts, tp=ts, tq=None, hps=None, tk=2048, L=ts, pairs=8,
                 hl_rows=bs * ts)
    smk = bf(cache_mem_k.reshape(depth, bs, nmem, d))
    smv = bf(cache_mem_v.reshape(depth, bs, nmem, d))
    outs_s = _trunk(x_sample, past, cache_diff_k, cache_diff_v, state_pool, state_rw_shift, state_rw_wkv,
                    smk, smv, W, cfg_s)

    y_p, pk, pv, pp, psh, pS = outs_p
    y_s, sk, sv, sp, ssh, sS = outs_s
    return (y_p, y_s, pk, pv, pp, psh, pS, p_mem_k, p_mem_v, sk, sv, sp, ssh, sS)
```

```python
import functools
import math

import jax
import jax.numpy as jnp
import numpy as np
from jax import lax
from jax.experimental import pallas as pl
from jax.experimental.pallas import tpu as pltpu

F32 = jnp.float32
BF16 = jnp.bfloat16

CHUNK = 64
POOL_GROUPS = 4
POOL_WINDOWS = (2, 4, 8, 16)
POOL_HIST = max(POOL_WINDOWS) - 1
POOL_HALO = 16
DIFF_DH = 64
DIFF_VD = 2 * DIFF_DH
RW_N = 64
RW_LN_EPS = 64e-5
XA_HEADS = 4
NORM_EPS = 1e-6
NEG_INF = -1e30
LOG2E = math.log2(math.e)
LANES = 128

VMEM_LIMIT = 56 * 1024 * 1024


def _cparams(sem):
    return pltpu.CompilerParams(dimension_semantics=sem, vmem_limit_bytes=VMEM_LIMIT)


def _layer_spec(arr, layer, block=None, index=None):
    block = tuple(arr.shape[1:]) if block is None else tuple(block)
    index = (0,) * len(block) if index is None else tuple(index)
    return pl.BlockSpec((None,) + block, lambda *_: (layer,) + index, pipeline_mode=pl.Buffered(1))


def _bdot(a, b):
    return jnp.dot(a.astype(BF16), b.astype(BF16), preferred_element_type=F32)


def _bdot_nt(a, b):
    return lax.dot_general(a.astype(BF16), b.astype(BF16), (((1,), (1,)), ((), ())),
                           preferred_element_type=F32)


def _bdot_tn(a, b):
    return lax.dot_general(a.astype(BF16), b.astype(BF16), (((0,), (0,)), ((), ())),
                           preferred_element_type=F32)


def _rms(x, g):
    return x * lax.rsqrt(jnp.mean(x * x, axis=-1, keepdims=True) + NORM_EPS) * g


def _sigmoid(x):
    return 1.0 / (1.0 + jnp.exp(-x))


def _log2(c):
    assert c > 0 and c & (c - 1) == 0, c
    return c.bit_length() - 1


def _div_pow2(x, c):
    return lax.shift_right_arithmetic(x, _log2(c))


def _mod_pow2(x, c):
    assert c & (c - 1) == 0, c
    return x & (c - 1)


def _memkv_kernel(x_ref, w_ref, o_ref, ob_ref):
    y = _bdot(x_ref[...], w_ref[0])
    o_ref[0] = y
    ob_ref[0] = y.astype(BF16)


def _mem_kv(mem2d, w_stack):
    n, d = mem2d.shape
    nw = w_stack.shape[0]
    return pl.pallas_call(
        _memkv_kernel,
        grid=(nw,),
        in_specs=[pl.BlockSpec((n, d), lambda i: (0, 0)),
                  pl.BlockSpec((1, d, d), lambda i: (i, 0, 0))],
        out_specs=[pl.BlockSpec((1, n, d), lambda i: (i, 0, 0)),
                   pl.BlockSpec((1, n, d), lambda i: (i, 0, 0))],
        out_shape=[jax.ShapeDtypeStruct((nw, n, d), F32),
                   jax.ShapeDtypeStruct((nw, n, d), BF16)],
        compiler_params=_cparams(("parallel",)),
        name="mem_kv",
    )(mem2d, w_stack)


def _pre_even_kernel(x_ref, g_ref, w_ref, u_ref, k_ref, v_ref, *rest, pw, dw, transposed):
    h = _rms(x_ref[...], g_ref[...])
    z = _bdot(h, w_ref[...])
    tm = z.shape[0]
    heads = dw // LANES
    u_ref[...] = z[:, :pw]
    q = z[:, pw:pw + dw] * (DIFF_DH ** -0.5)
    k = z[:, pw + dw:pw + 2 * dw]
    v = z[:, pw + 2 * dw:]
    for hd in range(heads):
        sl = slice(hd * LANES, (hd + 1) * LANES)
        k_ref[pl.ds(hd, tm, stride=heads), :] = k[:, sl]
        v_ref[pl.ds(hd, tm, stride=heads), :] = v[:, sl]
    if transposed:
        kb_ref, qt_ref, vt_ref = rest
        kb_ref[...] = k.astype(BF16)
        for hd in range(heads):
            sl = slice(hd * LANES, (hd + 1) * LANES)
            qt_ref[0, hd] = (q[:, sl] * LOG2E).T.astype(BF16)
            vt_ref[0, hd] = v[:, sl].T.astype(BF16)
    else:
        (q_ref,) = rest
        q_ref[...] = q.astype(BF16)


def _pre_even(x2d, g, w_in, layer, e, tm, seq_len, transposed):
    n, d = x2d.shape
    pw = d // 2
    dw = d // 2
    heads = dw // LANES
    row = lambda c: pl.BlockSpec((tm, c), lambda i: (i, 0))
    kv_rows = pl.BlockSpec((tm * heads, LANES), lambda i: (i, 0))
    out_specs = [row(pw), kv_rows, kv_rows]
    out_shape = [jax.ShapeDtypeStruct((n, pw), F32), jax.ShapeDtypeStruct((n * heads, LANES), F32),
                 jax.ShapeDtypeStruct((n * heads, LANES), F32)]
    if transposed:
        tps = seq_len // tm
        tspec = pl.BlockSpec((1, heads, LANES, tm), lambda i: (i // tps, 0, 0, i % tps))
        tshape = jax.ShapeDtypeStruct((n // seq_len, heads, LANES, seq_len), BF16)
        out_specs += [row(dw), tspec, tspec]
        out_shape += [jax.ShapeDtypeStruct((n, dw), BF16), tshape, tshape]
    else:
        out_specs += [row(dw)]
        out_shape += [jax.ShapeDtypeStruct((n, dw), BF16)]
    return pl.pallas_call(
        functools.partial(_pre_even_kernel, pw=pw, dw=dw, transposed=transposed),
        grid=(n // tm,),
        in_specs=[row(d), _layer_spec(g, layer), _layer_spec(w_in, e)],
        out_specs=out_specs,
        out_shape=out_shape,
        compiler_params=_cparams(("parallel",)),
        name="pre_even",
    )(x2d, g, w_in)


def _pool_kernel(u_ref, uprev_ref, hist_ref, w_ref, sc_ref, o_ref, *, tp, pos0, gw):
    t = pl.program_id(1)
    u = u_ref[0]
    prev = jnp.where(t == 0, hist_ref[0], uprev_ref[0])
    full = jnp.concatenate([prev, u], axis=0)
    sums = []
    s = full
    for sh in (1, 2, 4, 8):
        s = s + pltpu.roll(s, sh, 0)
        sums.append(s)
    pos = pos0 + t * tp + lax.broadcasted_iota(jnp.int32, (tp, 1), 0)
    for g, w in enumerate(POOL_WINDOWS):
        sl = slice(g * gw, (g + 1) * gw)
        cnt = jnp.minimum(pos + 1, w).astype(F32)
        pooled = sums[g][POOL_HALO:, sl] / cnt - u[:, sl]
        y = _bdot(pooled, w_ref[g]) * sc_ref[:, sl]
        o_ref[0, :, sl] = y.astype(BF16)


def _pool(u3, hist16, pool_w, scale, e, tp, pos0):
    b, t, pw = u3.shape
    gw = pw // POOL_GROUPS
    hb = tp // POOL_HALO
    return pl.pallas_call(
        functools.partial(_pool_kernel, tp=tp, pos0=pos0, gw=gw),
        grid=(b, t // tp),
        in_specs=[pl.BlockSpec((1, tp, pw), lambda i, j: (i, j, 0)),
                  pl.BlockSpec((1, POOL_HALO, pw), lambda i, j: (i, jnp.maximum(j * hb - 1, 0), 0)),
                  pl.BlockSpec((1, POOL_HALO, pw), lambda i, j: (i, 0, 0)),
                  _layer_spec(pool_w, e), _layer_spec(scale, e)],
        out_specs=pl.BlockSpec((1, tp, pw), lambda i, j: (i, j, 0)),
        out_shape=jax.ShapeDtypeStruct((b, t, pw), BF16),
        compiler_params=_cparams(("parallel", "parallel")),
        name="pool_mixer",
    )(u3, u3, hist16, pool_w, scale)


def _lam(lq1, lk1, lq2, lk2, lam_init):
    return (jnp.exp(jnp.sum(lq1[...] * lk1[...], axis=-1, keepdims=True))
            - jnp.exp(jnp.sum(lq2[...] * lk2[...], axis=-1, keepdims=True)) + lam_init)


def _stack_maps(q):
    lane = lax.broadcasted_iota(jnp.int32, q.shape, 1)
    zero = jnp.zeros_like(q)
    return jnp.concatenate([jnp.where(lane < DIFF_DH, q, zero), jnp.where(lane >= DIFF_DH, q, zero)], axis=0)


def _softmax_step(s, v, m_prev, l_prev, acc_prev):
    m_new = jnp.maximum(m_prev, jnp.max(s, axis=-1, keepdims=True))
    alpha = jnp.exp(m_prev - m_new)
    p = jnp.exp(s - m_new)
    l_new = alpha * l_prev + jnp.sum(p, axis=-1, keepdims=True)
    acc_new = alpha * acc_prev + _bdot(p, v)
    return m_new, l_new, acc_new


def _diff_finish(l, acc, lam, subg, rows, out_scale):
    o = acc[:rows] / l[:rows] - lam * (acc[rows:] / l[rows:])
    o = o * lax.rsqrt(jnp.mean(o * o, axis=-1, keepdims=True) + NORM_EPS) * subg
    return o * out_scale


def _diffp_kernel(qi_ref, kj_ref, qt_ref, k_ref, vt_ref, bias_ref, lq1, lk1, lq2, lk2, subg_ref, o_ref,
                  q2_ref, m_ref, l_ref, acc_ref, *, tq, hps, lam_init):
    step_id = pl.program_id(2)
    i = qi_ref[step_id]
    j = kj_ref[step_id]

    @pl.when(j == 0)
    def _():
        for hh in range(hps):
            qt = qt_ref[0, hh]
            row = lax.broadcasted_iota(jnp.int32, qt.shape, 0)
            zero = jnp.zeros_like(qt)
            q2_ref[hh, :, :tq] = jnp.where(row < DIFF_DH, qt, zero)
            q2_ref[hh, :, tq:] = jnp.where(row >= DIFF_DH, qt, zero)
        m_ref[...] = jnp.full(m_ref.shape, NEG_INF, F32)
        l_ref[...] = jnp.zeros(l_ref.shape, F32)
        acc_ref[...] = jnp.zeros(acc_ref.shape, F32)

    def step(masked):
        for hh in range(hps):
            s = jnp.dot(k_ref[0, :, hh * LANES:(hh + 1) * LANES], q2_ref[hh],
                        preferred_element_type=F32)
            if masked:
                s = s + bias_ref[...]
            m_prev = m_ref[hh]
            m_new = jnp.maximum(m_prev, jnp.max(s, axis=0, keepdims=True))
            alpha = jnp.exp2(m_prev - m_new)
            p = jnp.exp2(s - m_new)
            l_ref[hh] = alpha * l_ref[hh] + jnp.sum(p, axis=0, keepdims=True)
            acc_ref[hh] = alpha * acc_ref[hh] + jnp.dot(vt_ref[0, hh], p.astype(BF16),
                                                        preferred_element_type=F32)
            m_ref[hh] = m_new

    @pl.when(j < i)
    def _():
        step(False)

    @pl.when(j == i)
    def _():
        step(True)
        lam = _lam(lq1, lk1, lq2, lk2, lam_init)
        for hh in range(hps):
            acc = acc_ref[hh]
            l = l_ref[hh]
            ot = acc[:, :tq] / l[:, :tq] - lam * (acc[:, tq:] / l[:, tq:])
            ot = ot * lax.rsqrt(jnp.mean(ot * ot, axis=0, keepdims=True) + NORM_EPS)
            o_ref[0, :, hh * LANES:(hh + 1) * LANES] = (ot.T * subg_ref[...] * (1.0 - lam_init)).astype(BF16)


def _diff_prompt(qt, kb3, vt, lam_vecs, subg, e, lam_init, tq, hps):
    b, t, w = kb3.shape
    heads = w // LANES
    nq = t // tq
    pairs = [(i, j) for i in range(nq) for j in range(i + 1)]
    qi = jnp.asarray(np.array([p[0] for p in pairs], np.int32))
    kj = jnp.asarray(np.array([p[1] for p in pairs], np.int32))
    key_chunk = np.arange(tq)[:, None] // CHUNK
    query_chunk = (np.arange(2 * tq)[None, :] % tq) // CHUNK
    bias = jnp.asarray(np.where(key_chunk <= query_chunk, 0.0, NEG_INF).astype(np.float32))
    qspec = pl.BlockSpec((1, hps, LANES, tq), lambda bb, h, s, qi_r, kj_r: (bb, h, 0, qi_r[s]))
    kspec = pl.BlockSpec((1, tq, hps * LANES), lambda bb, h, s, qi_r, kj_r: (bb, kj_r[s], h))
    vspec = pl.BlockSpec((1, hps, LANES, tq), lambda bb, h, s, qi_r, kj_r: (bb, h, 0, kj_r[s]))
    vec = lambda a: pl.BlockSpec((None,) + tuple(a.shape[1:]), lambda *_: (e, 0, 0))
    grid_spec = pltpu.PrefetchScalarGridSpec(
        num_scalar_prefetch=2,
        grid=(b, heads // hps, len(pairs)),
        in_specs=[qspec, kspec, vspec,
                  pl.BlockSpec(bias.shape, lambda *_: (0, 0), pipeline_mode=pl.Buffered(1))]
        + [vec(a) for a in lam_vecs] + [vec(subg)],
        out_specs=pl.BlockSpec((1, tq, hps * LANES), lambda bb, h, s, qi_r, kj_r: (bb, qi_r[s], h)),
        scratch_shapes=[pltpu.VMEM((hps, LANES, 2 * tq), BF16), pltpu.VMEM((hps, 1, 2 * tq), F32),
                        pltpu.VMEM((hps, 1, 2 * tq), F32), pltpu.VMEM((hps, LANES, 2 * tq), F32)],
    )
    return pl.pallas_call(
        functools.partial(_diffp_kernel, tq=tq, hps=hps, lam_init=lam_init),
        grid_spec=grid_spec,
        out_shape=jax.ShapeDtypeStruct((b, t, w), BF16),
        compiler_params=_cparams(("parallel", "parallel", "arbitrary")),
        name="diff_attn_prompt",
    )(qi, kj, qt, kb3, vt, bias, *lam_vecs, subg)


def _diffs_kernel(q_ref, ck_ref, cv_ref, kn_ref, vn_ref, lq1, lk1, lq2, lk2, subg_ref, o_ref,
                  m_ref, l_ref, acc_ref, *, ts, heads, lam_init):
    j = pl.program_id(1)
    nj = pl.num_programs(1)
    nq = heads * 2 * ts
    q2 = jnp.concatenate([_stack_maps(q_ref[0, :, h * LANES:(h + 1) * LANES]) for h in range(heads)], axis=0)

    def visible(ncols, valid_cols):
        qh = _div_pow2(lax.broadcasted_iota(jnp.int32, (nq, ncols), 0), 2 * ts)
        col = lax.broadcasted_iota(jnp.int32, (nq, ncols), 1)
        ok = _mod_pow2(col, heads) == qh
        return ok if valid_cols is None else ok & (col < valid_cols)

    @pl.when(j == 0)
    def _():
        nnew = kn_ref.shape[1]
        pad = jnp.zeros((LANES - nnew, LANES), F32)
        kn = jnp.concatenate([kn_ref[0], pad], axis=0)
        vn = jnp.concatenate([vn_ref[0], pad], axis=0)
        s = jnp.where(visible(LANES, nnew), _bdot_nt(q2, kn), NEG_INF)
        m, l, acc = _softmax_step(s, vn, jnp.full((nq, 1), NEG_INF, F32), jnp.zeros((nq, 1), F32),
                                  jnp.zeros((nq, LANES), F32))
        m_ref[...] = m
        l_ref[...] = l
        acc_ref[...] = acc

    s = _bdot_nt(q2, ck_ref[...])
    s = jnp.where(visible(s.shape[1], None), s, NEG_INF)
    m, l, acc = _softmax_step(s, cv_ref[...], m_ref[...], l_ref[...], acc_ref[...])
    m_ref[...] = m
    l_ref[...] = l
    acc_ref[...] = acc

    @pl.when(j == nj - 1)
    def _():
        lam = _lam(lq1, lk1, lq2, lk2, lam_init)
        l_all = l_ref[...]
        acc_all = acc_ref[...]
        for h in range(heads):
            rows = slice(h * 2 * ts, (h + 1) * 2 * ts)
            o_ref[0, :, h * LANES:(h + 1) * LANES] = _diff_finish(
                l_all[rows], acc_all[rows], lam, subg_ref[...], ts, 1.0 - lam_init).astype(BF16)


def _diff_sample(q3, ck5, cv5, kn_rows, vn_rows, lam_vecs, subg, e, lam_init, tk):
    b, ts, w = q3.shape
    nl, _, past, heads, _ = ck5.shape
    assert heads * LANES == w and heads * ts <= LANES
    assert past % CHUNK == 0 and ts <= CHUNK and past % tk == 0
    rows = lambda a: a.reshape(a.shape[:-3] + (a.shape[-3] * heads, LANES))
    qspec = pl.BlockSpec((1, ts, w), lambda bb, j: (bb, 0, 0))
    new = pl.BlockSpec((1, ts * heads, LANES), lambda bb, j: (bb, 0, 0))
    cache = pl.BlockSpec((None, None, tk * heads, LANES), lambda bb, j: (e, bb, j, 0))
    vec = lambda a: pl.BlockSpec((None,) + tuple(a.shape[1:]), lambda *_: (e, 0, 0))
    nq = heads * 2 * ts
    return pl.pallas_call(
        functools.partial(_diffs_kernel, ts=ts, heads=heads, lam_init=lam_init),
        grid=(b, past // tk),
        in_specs=[qspec, cache, cache, new, new] + [vec(a) for a in lam_vecs] + [vec(subg)],
        out_specs=qspec,
        out_shape=jax.ShapeDtypeStruct((b, ts, w), BF16),
        scratch_shapes=[pltpu.VMEM((nq, 1), F32), pltpu.VMEM((nq, 1), F32), pltpu.VMEM((nq, LANES), F32)],
        compiler_params=_cparams(("parallel", "arbitrary")),
        name="diff_attn_sample",
    )(q3, rows(ck5), rows(cv5), kn_rows, vn_rows, *lam_vecs, subg)


def _mixxa_compute(x_ref, m1_ref, m2_ref, w1_ref, w2_ref, g_ref, wq_ref, mk_ref, mv_ref, wo_ref, *, nseq, heads):
    x1 = x_ref[...] + _bdot(m1_ref[...], w1_ref[...]) + _bdot(m2_ref[...], w2_ref[...])
    tm, d = x1.shape
    dh = d // heads
    h = _rms(x1, g_ref[...])
    q = (_bdot(h, wq_ref[...]) * (dh ** -0.5)).astype(BF16)
    rq = tm // nseq
    if len(mk_ref.shape) == 4:
        nrows = mk_ref.shape[1] * heads
        qh = _div_pow2(lax.broadcasted_iota(jnp.int32, (heads * rq, nrows), 0), rq)
        own = _mod_pow2(lax.broadcasted_iota(jnp.int32, (heads * rq, nrows), 1), heads) == qh
        outs = []
        for b in range(nseq):
            qb = q[b * rq:(b + 1) * rq]
            qcat = jnp.concatenate([qb[:, hd * dh:(hd + 1) * dh] for hd in range(heads)], axis=0)
            s = jnp.where(own, _bdot_nt(qcat, mk_ref[b].reshape(nrows, dh)), NEG_INF)
            p = jnp.exp(s - jnp.max(s, axis=-1, keepdims=True))
            p = p / jnp.sum(p, axis=-1, keepdims=True)
            ob = _bdot(p, mv_ref[b].reshape(nrows, dh))
            outs.append(jnp.concatenate([ob[hd * rq:(hd + 1) * rq] for hd in range(heads)], axis=1))
        o = jnp.concatenate(outs, axis=0)
    else:
        q3 = q.reshape(nseq, rq, d)
        outs = []
        for hd in range(heads):
            sl = slice(hd * dh, (hd + 1) * dh)
            s = jnp.einsum("bqd,bkd->bqk", q3[:, :, sl], mk_ref[:, :, sl], preferred_element_type=F32)
            p = jnp.exp(s - jnp.max(s, axis=-1, keepdims=True))
            p = p / jnp.sum(p, axis=-1, keepdims=True)
            outs.append(jnp.einsum("bqk,bkd->bqd", p.astype(BF16), mv_ref[:, :, sl], preferred_element_type=F32))
        o = jnp.concatenate(outs, axis=-1).reshape(tm, d)
    return x1 + _bdot(o, wo_ref[...])


def _mixxa_kernel(*refs, nseq, heads):
    refs[-1][...] = _mixxa_compute(*refs[:-1], nseq=nseq, heads=heads)


def _mixxa_ffn_kernel(*refs, nseq, heads, nchunk, final):
    x2 = _mixxa_compute(*refs[:10], nseq=nseq, heads=heads)
    refs[-1][...] = _ffn_compute(x2, *refs[10:-1], nchunk=nchunk, final=final)


FFN_CHUNKS = 2


def _mix_xa(x2d, m1, m2, w_mix, lmix, g, wq, mk, mv, wo, layer, tm, rows_per_seq, ffn=None):
    n, d = x2d.shape
    half = d // 2
    nseq = max(tm // rows_per_seq, 1)
    tiles_per_seq = max(rows_per_seq // tm, 1)
    row = lambda c: pl.BlockSpec((tm, c), lambda i: (i, 0))
    col = lambda c: pl.BlockSpec((tm, half), lambda i: (i, c))
    tail = (0,) * (mk.ndim - 2)
    mem = pl.BlockSpec((None, nseq) + mk.shape[2:], lambda i: (layer, i // tiles_per_seq) + tail)
    ins = [x2d, m1[0], m2[0], w_mix, w_mix, g, wq, mk, mv, wo]
    specs = [row(d), col(m1[1]), col(m2[1]),
             _layer_spec(w_mix, lmix, (half, d), (0, 0)), _layer_spec(w_mix, lmix, (half, d), (1, 0)),
             _layer_spec(g, layer), _layer_spec(wq, layer), mem, mem, _layer_spec(wo, layer)]
    if ffn is None:
        body = functools.partial(_mixxa_kernel, nseq=nseq, heads=XA_HEADS)
    else:
        gf, wg, wu, wd, gfin, final = ffn
        ins += [gf, wg, wu, wd, gfin]
        specs += [_layer_spec(a, layer) for a in (gf, wg, wu, wd)] + [pl.BlockSpec((1, d), lambda i: (0, 0))]
        body = functools.partial(_mixxa_ffn_kernel, nseq=nseq, heads=XA_HEADS, nchunk=FFN_CHUNKS, final=final)
    return pl.pallas_call(
        body,
        grid=(n // tm,),
        in_specs=specs,
        out_specs=row(d),
        out_shape=jax.ShapeDtypeStruct((n, d), F32),
        compiler_params=_cparams(("parallel",)),
        name="mix_xa" if ffn is None else "mix_xa_ffn",
    )(*ins)


def _ffn_compute(x, g_ref, wg_ref, wu_ref, wd_ref, gf_ref, *, nchunk, final):
    h = _rms(x, g_ref[...]).astype(BF16)
    ff = wg_ref.shape[1]
    fc = ff // nchunk
    acc = x
    for c in range(nchunk):
        sl = slice(c * fc, (c + 1) * fc)
        a = jnp.dot(h, wg_ref[:, sl], preferred_element_type=F32)
        b = jnp.dot(h, wu_ref[:, sl], preferred_element_type=F32)
        acc = acc + _bdot(a * _sigmoid(a) * b, wd_ref[sl, :])
    if final:
        acc = _rms(acc, gf_ref[...])
    return acc


def _ffn_kernel(x_ref, *refs, nchunk, final):
    refs[-1][...] = _ffn_compute(x_ref[...], *refs[:-1], nchunk=nchunk, final=final)


def _ffn(x2d, g, wg, wu, wd, gf, layer, tm, final):
    n, d = x2d.shape
    row = pl.BlockSpec((tm, d), lambda i: (i, 0))
    return pl.pallas_call(
        functools.partial(_ffn_kernel, nchunk=FFN_CHUNKS, final=final),
        grid=(n // tm,),
        in_specs=[row, _layer_spec(g, layer), _layer_spec(wg, layer), _layer_spec(wu, layer),
                  _layer_spec(wd, layer), pl.BlockSpec((1, d), lambda i: (0, 0))],
        out_specs=row,
        out_shape=jax.ShapeDtypeStruct((n, d), F32),
        compiler_params=_cparams(("parallel",)),
        name="ffn",
    )(x2d, g, wg, wu, wd, gf)


_RW_PARAMS = ("mu", "wr", "wk", "wv", "w0", "w1", "w2", "a0", "a1", "a2", "g1", "g2", "k_k", "k_a")


def _pre_rwkv_kernel(*refs, tm, rows_per_seq, hl_rows, has_vres):
    (x_ref, xp_ref, first_ref, g_ref, mu_ref, wr_ref, wk_ref, wv_ref, w0_ref, w1_ref, w2_ref,
     a0_ref, a1_ref, a2_ref, g1_ref, g2_ref, kk_ref, ka_ref) = refs[:18]
    pos = 18
    if has_vres:
        vf_ref, v0_ref, v1_ref, v2_ref = refs[pos:pos + 4]
        pos += 4
    r_ref, k_ref, v_ref, kr_ref, ag_ref, w_ref, gate_ref, hl_ref = refs[pos:]

    i = pl.program_id(0)
    g = g_ref[...]
    h = _rms(x_ref[...], g)
    hp_last = _rms(xp_ref[...], g)[xp_ref.shape[0] - 1:, :]
    rowl = lax.broadcasted_iota(jnp.int32, (tm, 1), 0)
    shifted = jnp.where(rowl == 0, hp_last, pltpu.roll(h, 1, 0))
    is_start = _mod_pow2(i * tm + rowl, rows_per_seq) == 0
    h_prev = jnp.where(is_start, first_ref[...], shifted)
    xx = h_prev - h
    mix = lambda n: (h + xx * mu_ref[n:n + 1, :]).astype(BF16)
    xr, xw, xk, xv, xa, xg = (mix(n) for n in range(6))

    r_ref[...] = _bdot(xr, wr_ref[...]).astype(r_ref.dtype)
    z = w0_ref[...] + _bdot(jnp.tanh(_bdot(xw, w1_ref[...])), w2_ref[...])
    w_ref[...] = -(jnp.maximum(-z, 0.0) + jnp.log1p(jnp.exp(-jnp.abs(z)))) - 0.5
    k = _bdot(xk, wk_ref[...])
    v = _bdot(xv, wv_ref[...])
    if has_vres:
        v = v + (vf_ref[...].astype(F32) - v) * _sigmoid(v0_ref[...] + _bdot(_bdot(xv, v1_ref[...]), v2_ref[...]))
    v_ref[...] = v.astype(v_ref.dtype)
    a = _sigmoid(a0_ref[...] + _bdot(_bdot(xa, a1_ref[...]), a2_ref[...]))
    ag_ref[...] = a.astype(ag_ref.dtype)
    gate_ref[...] = _bdot(_sigmoid(_bdot(xg, g1_ref[...])), g2_ref[...]).astype(gate_ref.dtype)
    kr_ref[...] = (k * kk_ref[...]).astype(kr_ref.dtype)
    k_ref[...] = (k * (1.0 + (a - 1.0) * ka_ref[...])).astype(k_ref.dtype)
    hl_ref[...] = h[tm - hl_rows:, :]


def _pre_rwkv(x2d, first, g, W, layer, o, vf, tm, rows_per_seq, hl_rows):
    n, d = x2d.shape
    has_vres = vf is not None
    xp_rows = 8
    row = pl.BlockSpec((tm, d), lambda i: (i, 0))
    ins = [x2d, x2d, first, g] + [W["rw_" + nm] for nm in _RW_PARAMS]
    specs = [row,
             pl.BlockSpec((xp_rows, d), lambda i: (jnp.maximum(i * (tm // xp_rows) - 1, 0), 0)),
             pl.BlockSpec((tm, d), lambda i: (i // max(rows_per_seq // tm, 1), 0)),
             _layer_spec(g, layer)] + [_layer_spec(W["rw_" + nm], o) for nm in _RW_PARAMS]
    if has_vres:
        vres = [W["rw_v0"], W["rw_v1"], W["rw_v2"]]
        ins += [vf] + vres
        specs += [row] + [_layer_spec(a, o - 1) for a in vres]
    nt = n // tm
    dts = [BF16, BF16, BF16, BF16, BF16, F32, BF16]
    return pl.pallas_call(
        functools.partial(_pre_rwkv_kernel, tm=tm, rows_per_seq=rows_per_seq, hl_rows=hl_rows,
                          has_vres=has_vres),
        grid=(nt,),
        in_specs=specs,
        out_specs=[row] * 7 + [pl.BlockSpec((hl_rows, d), lambda i: (i, 0))],
        out_shape=[jax.ShapeDtypeStruct((n, d), dt) for dt in dts]
        + [jax.ShapeDtypeStruct((nt * hl_rows, d), F32)],
        compiler_params=_cparams(("parallel",)),
        name="pre_rwkv",
    )(*ins)


def _split3(x):
    x1 = x.astype(BF16)
    r1 = x - x1.astype(F32)
    x2 = r1.astype(BF16)
    x3 = (r1 - x2.astype(F32)).astype(BF16)
    return x1, x2, x3


def _head_sum(x, lo_mask):
    s_lo = jnp.sum(jnp.where(lo_mask, x, 0.0), axis=-1, keepdims=True)
    s_hi = jnp.sum(jnp.where(lo_mask, 0.0, x), axis=-1, keepdims=True)
    return jnp.where(lo_mask, s_lo, s_hi)


def _rwkv_rec_kernel(r_ref, k_ref, v_ref, kr_ref, ag_ref, w_ref, gate_ref, lng_ref, lnb_ref, rk_ref, s0_ref,
                     y_ref, sfin_ref, s_scr, *, L, pairs):
    c = pl.program_id(2)
    nc = pl.num_programs(2)
    P = range(pairs)

    @pl.when(c == 0)
    def _():
        zero = jnp.zeros((RW_N, RW_N), F32)
        for p in P:
            top = jnp.concatenate([s0_ref[0, 2 * p], zero], axis=1)
            bot = jnp.concatenate([zero, s0_ref[0, 2 * p + 1]], axis=1)
            s_scr[p] = jnp.concatenate([top, bot], axis=0)

    L2 = 2 * L
    sub = min(16, L)
    nblk = L // sub
    lane = lax.broadcasted_iota(jnp.int32, (L, LANES), 1)
    lo = lane < RW_N
    row2 = lax.broadcasted_iota(jnp.int32, (L2, L2), 0)
    col2 = lax.broadcasted_iota(jnp.int32, (L2, L2), 1)
    same = _div_pow2(row2, L) == _div_pow2(col2, L)
    strict = same & (col2 < row2)
    incl = same & (col2 <= row2)
    diag_blk = _div_pow2(row2, sub) == _div_pow2(col2, sub)
    eye = (row2 == col2).astype(F32)
    tri = (lax.broadcasted_iota(jnp.int32, (L, L), 1) <= lax.broadcasted_iota(jnp.int32, (L, L), 0)).astype(BF16)
    dot = lambda a, b: jnp.dot(a, b, preferred_element_type=F32)

    def stack(x):
        xb = x.astype(BF16)
        zero = jnp.zeros_like(xb)
        return jnp.concatenate([jnp.where(lo, xb, zero), jnp.where(lo, zero, xb)], axis=0)

    sl = [slice(p * LANES, (p + 1) * LANES) for p in P]
    r = [r_ref[0, :, sl[p]].astype(F32) for p in P]
    k = [k_ref[0, :, sl[p]].astype(F32) for p in P]
    v = [v_ref[0, :, sl[p]].astype(F32) for p in P]
    logd = [-jnp.exp(w_ref[0, :, sl[p]]) for p in P]
    kk = []
    for p in P:
        kr = kr_ref[0, :, sl[p]].astype(F32)
        kk.append(kr / jnp.maximum(jnp.sqrt(_head_sum(kr * kr, lo)), 1e-12))
    cum = []
    for p in P:
        l1, l2, l3 = _split3(logd[p])
        cum.append(dot(tri, l1) + dot(tri, l2) + dot(tri, l3))
    ar_s, bk_s, v_s, bkh_s, p_last = [], [], [], [], []
    for p in P:
        e_pos = jnp.exp(cum[p])
        e_neg = jnp.exp(-cum[p])
        pl_ = e_pos[L - 1:, :]
        bt = kk[p] * ag_ref[0, :, sl[p]].astype(F32) * e_neg
        kt = k[p] * e_neg
        ar_s.append(jnp.concatenate([stack(-kk[p] * jnp.exp(cum[p] - logd[p])), stack(r[p] * e_pos)], axis=0))
        bk_s.append(jnp.concatenate([stack(bt), stack(kt)], axis=0))
        bkh_s.append(jnp.concatenate([stack(bt * pl_), stack(kt * pl_)], axis=0))
        v_s.append(stack(v[p]))
        p_last.append(pl_)
    s_bd = [s_scr[p] for p in P]
    ss = [_bdot_nt(ar_s[p], s_bd[p]) for p in P]
    g_all = [_bdot_nt(ar_s[p], bk_s[p]) for p in P]
    a_ab = [jnp.where(strict, g_all[p][:L2, :L2], 0.0) for p in P]
    a_akrk = [jnp.concatenate([jnp.where(strict, g_all[p][:L2, L2:], 0.0),
                               jnp.where(incl, g_all[p][L2:, L2:], 0.0)], axis=0) for p in P]
    a_rb = [jnp.where(incl, g_all[p][L2:, :L2], 0.0) for p in P]
    sv = [_bdot(a_akrk[p], v_s[p]) for p in P]
    wmat = [ss[p][:L2] + sv[p][:L2] for p in P]

    a_d = [jnp.where(diag_blk, a_ab[p], 0.0) for p in P]
    pm = [eye + a_d[p] for p in P]
    pw = a_d
    for _ in range(_log2(sub) - 1):
        pw = [_bdot(pw[p], pw[p]) for p in P]
        pm = [pm[p] + _bdot(pw[p], pm[p]) for p in P]
    if nblk > 1:
        xn = [_bdot(pm[p], jnp.concatenate([wmat[p], a_ab[p] - a_d[p]], axis=1)) for p in P]
        u = [xn[p][:, :LANES] for p in P]
        nm = [xn[p][:, LANES:] for p in P]
        steps = _log2(nblk)
        for it in range(steps):
            u = [u[p] + _bdot(nm[p], u[p]) for p in P]
            if it < steps - 1:
                nm = [_bdot(nm[p], nm[p]) for p in P]
    else:
        u = [_bdot(pm[p], wmat[p]) for p in P]

    y2 = [ss[p][L2:] + sv[p][L2:] + _bdot(a_rb[p], u[p]) for p in P]
    for p in P:
        uv = jnp.concatenate([u[p].astype(BF16), v_s[p]], axis=0)
        s_scr[p] = s_bd[p] * p_last[p] + _bdot_tn(uv, bkh_s[p])
    for p in P:
        y = y2[p][:L] + y2[p][L:]
        mean = _head_sum(y, lo) * (1.0 / RW_N)
        yc = y - mean
        var = _head_sum(yc * yc, lo) * (1.0 / RW_N)
        yn = yc * lax.rsqrt(var + RW_LN_EPS) * lng_ref[:, sl[p]] + lnb_ref[:, sl[p]]
        bonus = _head_sum(r[p] * k[p] * rk_ref[:, sl[p]], lo) * v[p]
        y_ref[0, :, sl[p]] = ((yn + bonus) * gate_ref[0, :, sl[p]].astype(F32)).astype(BF16)

    @pl.when(c == nc - 1)
    def _():
        for p in P:
            s = s_scr[p]
            sfin_ref[0, 2 * p] = s[:RW_N, :RW_N]
            sfin_ref[0, 2 * p + 1] = s[RW_N:, RW_N:]


def _rwkv_rec(arrs, lng, lnb, rk, s0, o, L, pairs):
    b, t, d = arrs[0].shape
    heads = d // RW_N
    groups = heads // (2 * pairs)
    w = pairs * LANES
    tok = pl.BlockSpec((1, L, w), lambda bb, gg, c: (bb, c, gg))
    vec = pl.BlockSpec((None, 1, w), lambda bb, gg, c: (o, 0, gg))
    st_in = pl.BlockSpec((None, 1, 2 * pairs, RW_N, RW_N), lambda bb, gg, c: (o, bb, gg, 0, 0))
    st_out = pl.BlockSpec((1, 2 * pairs, RW_N, RW_N), lambda bb, gg, c: (bb, gg, 0, 0))
    return pl.pallas_call(
        functools.partial(_rwkv_rec_kernel, L=L, pairs=pairs),
        grid=(b, groups, t // L),
        in_specs=[tok] * 7 + [vec, vec, vec, st_in],
        out_specs=[tok, st_out],
        out_shape=[jax.ShapeDtypeStruct((b, t, d), BF16),
                   jax.ShapeDtypeStruct((b, heads, RW_N, RW_N), F32)],
        scratch_shapes=[pltpu.VMEM((pairs, LANES, LANES), F32)],
        compiler_params=_cparams(("parallel", "parallel", "arbitrary")),
        name="rwkv_rec",
    )(*arrs, lng, lnb, rk, s0)


def _trunk(x, pos0, diff_k_past, diff_v_past, pool_hist, rw_shift, rw_state, mem_k, mem_v, W, cfg):
    b, t, d = x.shape
    n = b * t
    depth = W["norm_mix_g"].shape[0]
    tm, tm_rw, tm_xa = cfg["tm"], cfg["tm_rw"], cfg["tm_xa"]
    x2 = x.reshape(n, d)
    new_k, new_v, new_pool, new_shift, new_s = [], [], [], [], []
    v_first = None
    half = d // 2
    heads = half // LANES
    for l in range(depth):
        if l % 2 == 0:
            e = l // 2
            prompt = diff_k_past is None
            outs = _pre_even(x2, W["norm_mix_g"], W["ev_w_in"], l, e, tm, t, transposed=prompt)
            u, k_rows, v_rows = outs[:3]
            new_k.append(k_rows.reshape(b, t, heads, 2 * DIFF_DH))
            new_v.append(v_rows.reshape(b, t, heads, DIFF_VD))
            u3 = u.reshape(b, t, half)
            new_pool.append(u3[:, t - POOL_HIST:])
            hist16 = jnp.pad(pool_hist[e], ((0, 0), (POOL_HALO - POOL_HIST, 0), (0, 0)))
            m1 = _pool(u3, hist16, W["ev_pool_w"], W["ev_pool_scale"], e, cfg["tp"], pos0).reshape(n, half)
            lam_init = 0.8 - 0.6 * math.exp(-0.3 * l)
            lam_vecs = [W[nm] for nm in ("ev_lam_q1", "ev_lam_k1", "ev_lam_q2", "ev_lam_k2")]
            if prompt:
                kb, qt, vt = outs[3:]
                m2 = _diff_prompt(qt, kb.reshape(b, t, half), vt, lam_vecs, W["ev_subln_g"], e, lam_init,
                                  cfg["tq"], cfg["hps"])
            else:
                m2 = _diff_sample(outs[3].reshape(b, t, half), diff_k_past, diff_v_past,
                                  k_rows.reshape(b, t * heads, LANES), v_rows.reshape(b, t * heads, LANES),
                                  lam_vecs, W["ev_subln_g"], e, lam_init, cfg["tk"])
            m1, m2 = (m1, 0), (m2.reshape(n, half), 0)
            w_mix, lmix = W["ev_w_out"], e
        else:
            o = l // 2
            rows_first = jnp.broadcast_to(rw_shift[o][:, None, :], (b, min(t, tm_rw), d)).reshape(-1, d)
            outs = _pre_rwkv(x2, rows_first, W["norm_mix_g"], W, l, o, None if o == 0 else v_first, tm_rw, t,
                             cfg["hl_rows"])
            r, k, v, kr, ag, wl, gate, hl = outs
            if o == 0:
                v_first = v
            if cfg["hl_rows"] == tm_rw:
                new_shift.append(hl.reshape(b, t, d)[:, -1])
            else:
                per_seq = t // tm_rw
                new_shift.append(hl.reshape(b, per_seq, cfg["hl_rows"], d)[:, -1, -1])
            arrs = [a.reshape(b, t, d) for a in (r, k, v, kr, ag, wl, gate)]
            y, s_fin = _rwkv_rec(arrs, W["rw_lnx_g"], W["rw_lnx_b"], W["rw_r_k"], rw_state, o, cfg["L"],
                                 cfg["pairs"])
            new_s.append(s_fin)
            y2 = y.reshape(n, d)
            m1, m2 = (y2, 0), (y2, 1)
            w_mix, lmix = W["rw_wo"], o
        ffn = (W["norm_ffn_g"], W["ffn_wg"], W["ffn_wu"], W["ffn_wd"], W["final_norm_g"], l == depth - 1)
        if cfg["fuse_ffn"]:
            x2 = _mix_xa(x2, m1, m2, w_mix, lmix, W["norm_xa_g"], W["xa_wq"], mem_k, mem_v, W["xa_wo"], l, tm_xa,
                         t, ffn=ffn)
        else:
            x2 = _mix_xa(x2, m1, m2, w_mix, lmix, W["norm_xa_g"], W["xa_wq"], mem_k, mem_v, W["xa_wo"], l, tm_xa,
                         t)
            x2 = _ffn(x2, *ffn[:5], l, tm, final=ffn[5])
    return (x2.reshape(b, t, d), jnp.stack(new_k), jnp.stack(new_v), jnp.stack(new_pool),
            jnp.stack(new_shift), jnp.stack(new_s))


def kernel(x_prompt, x_sample, cache_diff_k, cache_diff_v, state_pool, state_rw_shift, state_rw_wkv, cache_mem_k, cache_mem_v, mem_prompt, norm_mix_g, norm_xa_g, norm_ffn_g, final_norm_g, ev_w_in, ev_pool_w, ev_pool_scale, ev_lam_q1, ev_lam_k1, ev_lam_q2, ev_lam_k2, ev_subln_g, ev_w_out, rw_mu, rw_wr, rw_wk, rw_wv, rw_wo, rw_w0, rw_w1, rw_w2, rw_a0, rw_a1, rw_a2, rw_v0, rw_v1, rw_v2, rw_g1, rw_g2, rw_k_k, rw_k_a, rw_r_k, rw_lnx_g, rw_lnx_b, xa_wq, xa_wk, xa_wv, xa_wo, ffn_wg, ffn_wu, ffn_wd):
    bf = lambda a: a.astype(BF16)
    vec = lambda a: a.reshape(a.shape[0], 1, -1)
    W = dict(
        norm_mix_g=vec(norm_mix_g), norm_xa_g=vec(norm_xa_g), norm_ffn_g=vec(norm_ffn_g),
        final_norm_g=final_norm_g.reshape(1, -1),
        ev_w_in=bf(ev_w_in), ev_pool_w=bf(ev_pool_w), ev_pool_scale=vec(ev_pool_scale),
        ev_lam_q1=vec(ev_lam_q1), ev_lam_k1=vec(ev_lam_k1), ev_lam_q2=vec(ev_lam_q2), ev_lam_k2=vec(ev_lam_k2),
        ev_subln_g=vec(ev_subln_g), ev_w_out=bf(ev_w_out),
        rw_mu=rw_mu, rw_wr=bf(rw_wr), rw_wk=bf(rw_wk), rw_wv=bf(rw_wv), rw_wo=bf(rw_wo),
        rw_w0=vec(rw_w0), rw_w1=bf(rw_w1), rw_w2=bf(rw_w2), rw_a0=vec(rw_a0), rw_a1=bf(rw_a1), rw_a2=bf(rw_a2),
        rw_v0=vec(rw_v0), rw_v1=bf(rw_v1), rw_v2=bf(rw_v2), rw_g1=bf(rw_g1), rw_g2=bf(rw_g2),
        rw_k_k=vec(rw_k_k), rw_k_a=vec(rw_k_a), rw_r_k=vec(rw_r_k),
        rw_lnx_g=vec(rw_lnx_g), rw_lnx_b=vec(rw_lnx_b),
        xa_wq=bf(xa_wq), xa_wo=bf(xa_wo), ffn_wg=bf(ffn_wg), ffn_wu=bf(ffn_wu), ffn_wd=bf(ffn_wd),
    )
    depth, d = norm_mix_g.shape
    bp, tp_len = x_prompt.shape[:2]
    bs, ts = x_sample.shape[:2]
    n_even = ev_w_in.shape[0]
    n_odd = rw_wr.shape[0]
    nmem = mem_prompt.shape[1]

    kv_f32, kv_bf = _mem_kv(mem_prompt.reshape(bp * nmem, d), bf(jnp.concatenate([xa_wk, xa_wv], axis=0)))
    p_mem_k = kv_f32[:depth].reshape(depth, bp, nmem, XA_HEADS, d // XA_HEADS)
    p_mem_v = kv_f32[depth:].reshape(depth, bp, nmem, XA_HEADS, d // XA_HEADS)
    pmk = kv_bf[:depth].reshape(depth, bp, nmem, d)
    pmv = kv_bf[depth:].reshape(depth, bp, nmem, d)

    cfg_p = dict(tm=512, tm_rw=256, tm_xa=512, tp=512, tq=512, hps=2, tk=None, L=64, pairs=8, hl_rows=8,
                 fuse_ffn=True)
    zero_pool = jnp.zeros((n_even, bp, POOL_HIST, d // 2), F32)
    zero_shift = jnp.zeros((n_odd, bp, d), F32)
    zero_wkv = jnp.zeros((n_odd, bp, d // RW_N, RW_N, RW_N), F32)
    outs_p = _trunk(x_prompt, 0, None, None, zero_pool, zero_shift, zero_wkv, pmk, pmv, W, cfg_p)

    past = cache_diff_k.shape[2]
    cfg_s = dict(tm=bs * ts, tm_rw=bs * ts, tm_xa=4 * ts, tp=ts, tq=None, hps=None, tk=2048, L=ts, pairs=8,
                 hl_rows=bs * ts, fuse_ffn=False)
    outs_s = _trunk(x_sample, past, cache_diff_k, cache_diff_v, state_pool, state_rw_shift, state_rw_wkv,
                    cache_mem_k, cache_mem_v, W, cfg_s)

    y_p, pk, pv, pp, psh, pS = outs_p
    y_s, sk, sv, sp, ssh, sS = outs_s
    return (y_p, y_s, pk, pv, pp, psh, pS, p_mem_k, p_mem_v, sk, sv, sp, ssh, sS)
```

```python
import functools
import math

import jax
import jax.numpy as jnp
import numpy as np
from jax import lax
from jax.experimental import pallas as pl
from jax.experimental.pallas import tpu as pltpu

F32 = jnp.float32
BF16 = jnp.bfloat16

CHUNK = 64
POOL_GROUPS = 4
POOL_WINDOWS = (2, 4, 8, 16)
POOL_HIST = max(POOL_WINDOWS) - 1
POOL_HALO = 16
DIFF_DH = 64
DIFF_VD = 2 * DIFF_DH
RW_N = 64
RW_LN_EPS = 64e-5
XA_HEADS = 4
NORM_EPS = 1e-6
NEG_INF = -1e30
LOG2E = math.log2(math.e)
LANES = 128

VMEM_LIMIT = 56 * 1024 * 1024


def _cparams(sem):
    return pltpu.CompilerParams(dimension_semantics=sem, vmem_limit_bytes=VMEM_LIMIT)


def _layer_spec(arr, layer, block=None, index=None):
    block = tuple(arr.shape[1:]) if block is None else tuple(block)
    index = (0,) * len(block) if index is None else tuple(index)
    return pl.BlockSpec((None,) + block, lambda *_: (layer,) + index, pipeline_mode=pl.Buffered(1))


def _bdot(a, b):
    return jnp.dot(a.astype(BF16), b.astype(BF16), preferred_element_type=F32)


def _bdot_nt(a, b):
    return lax.dot_general(a.astype(BF16), b.astype(BF16), (((1,), (1,)), ((), ())),
                           preferred_element_type=F32)


def _bdot_tn(a, b):
    return lax.dot_general(a.astype(BF16), b.astype(BF16), (((0,), (0,)), ((), ())),
                           preferred_element_type=F32)


def _rms(x, g):
    return x * lax.rsqrt(jnp.mean(x * x, axis=-1, keepdims=True) + NORM_EPS) * g


def _sigmoid(x):
    return 1.0 / (1.0 + jnp.exp(-x))


def _log2(c):
    assert c > 0 and c & (c - 1) == 0, c
    return c.bit_length() - 1


def _div_pow2(x, c):
    return lax.shift_right_arithmetic(x, _log2(c))


def _mod_pow2(x, c):
    assert c & (c - 1) == 0, c
    return x & (c - 1)


def _memkv_kernel(x_ref, w_ref, o_ref, ob_ref):
    y = _bdot(x_ref[...], w_ref[0])
    o_ref[0] = y
    ob_ref[0] = y.astype(BF16)


def _mem_kv(mem2d, w_stack):
    n, d = mem2d.shape
    nw = w_stack.shape[0]
    return pl.pallas_call(
        _memkv_kernel,
        grid=(nw,),
        in_specs=[pl.BlockSpec((n, d), lambda i: (0, 0)),
                  pl.BlockSpec((1, d, d), lambda i: (i, 0, 0))],
        out_specs=[pl.BlockSpec((1, n, d), lambda i: (i, 0, 0)),
                   pl.BlockSpec((1, n, d), lambda i: (i, 0, 0))],
        out_shape=[jax.ShapeDtypeStruct((nw, n, d), F32),
                   jax.ShapeDtypeStruct((nw, n, d), BF16)],
        compiler_params=_cparams(("parallel",)),
        name="mem_kv",
    )(mem2d, w_stack)


def _pre_even_kernel(x_ref, g_ref, w_ref, u_ref, k_ref, v_ref, *rest, pw, dw, transposed):
    h = _rms(x_ref[...], g_ref[...])
    z = _bdot(h, w_ref[...])
    tm = z.shape[0]
    heads = dw // LANES
    u_ref[...] = z[:, :pw]
    q = z[:, pw:pw + dw] * (DIFF_DH ** -0.5)
    k = z[:, pw + dw:pw + 2 * dw]
    v = z[:, pw + 2 * dw:]
    for hd in range(heads):
        sl = slice(hd * LANES, (hd + 1) * LANES)
        k_ref[pl.ds(hd, tm, stride=heads), :] = k[:, sl]
        v_ref[pl.ds(hd, tm, stride=heads), :] = v[:, sl]
    if transposed:
        kb_ref, qt_ref, vt_ref = rest
        kb_ref[...] = k.astype(BF16)
        for hd in range(heads):
            sl = slice(hd * LANES, (hd + 1) * LANES)
            qt_ref[0, hd] = (q[:, sl] * LOG2E).T.astype(BF16)
            vt_ref[0, hd] = v[:, sl].T.astype(BF16)
    else:
        (q_ref,) = rest
        q_ref[...] = q.astype(BF16)


def _pre_even(x2d, g, w_in, layer, e, tm, seq_len, transposed):
    n, d = x2d.shape
    pw = d // 2
    dw = d // 2
    heads = dw // LANES
    row = lambda c: pl.BlockSpec((tm, c), lambda i: (i, 0))
    kv_rows = pl.BlockSpec((tm * heads, LANES), lambda i: (i, 0))
    out_specs = [row(pw), kv_rows, kv_rows]
    out_shape = [jax.ShapeDtypeStruct((n, pw), F32), jax.ShapeDtypeStruct((n * heads, LANES), F32),
                 jax.ShapeDtypeStruct((n * heads, LANES), F32)]
    if transposed:
        tps = seq_len // tm
        tspec = pl.BlockSpec((1, heads, LANES, tm), lambda i: (i // tps, 0, 0, i % tps))
        tshape = jax.ShapeDtypeStruct((n // seq_len, heads, LANES, seq_len), BF16)
        out_specs += [row(dw), tspec, tspec]
        out_shape += [jax.ShapeDtypeStruct((n, dw), BF16), tshape, tshape]
    else:
        out_specs += [row(dw)]
        out_shape += [jax.ShapeDtypeStruct((n, dw), BF16)]
    return pl.pallas_call(
        functools.partial(_pre_even_kernel, pw=pw, dw=dw, transposed=transposed),
        grid=(n // tm,),
        in_specs=[row(d), _layer_spec(g, layer), _layer_spec(w_in, e)],
        out_specs=out_specs,
        out_shape=out_shape,
        compiler_params=_cparams(("parallel",)),
        name="pre_even",
    )(x2d, g, w_in)


def _pool_kernel(u_ref, uprev_ref, hist_ref, w_ref, sc_ref, o_ref, *, tp, pos0, gw):
    t = pl.program_id(1)
    u = u_ref[0]
    prev = jnp.where(t == 0, hist_ref[0], uprev_ref[0])
    full = jnp.concatenate([prev, u], axis=0)
    sums = []
    s = full
    for sh in (1, 2, 4, 8):
        s = s + pltpu.roll(s, sh, 0)
        sums.append(s)
    pos = pos0 + t * tp + lax.broadcasted_iota(jnp.int32, (tp, 1), 0)
    for g, w in enumerate(POOL_WINDOWS):
        sl = slice(g * gw, (g + 1) * gw)
        cnt = jnp.minimum(pos + 1, w).astype(F32)
        pooled = sums[g][POOL_HALO:, sl] / cnt - u[:, sl]
        y = _bdot(pooled, w_ref[g]) * sc_ref[:, sl]
        o_ref[0, :, sl] = y.astype(BF16)


def _pool(u3, hist16, pool_w, scale, e, tp, pos0):
    b, t, pw = u3.shape
    gw = pw // POOL_GROUPS
    hb = tp // POOL_HALO
    return pl.pallas_call(
        functools.partial(_pool_kernel, tp=tp, pos0=pos0, gw=gw),
        grid=(b, t // tp),
        in_specs=[pl.BlockSpec((1, tp, pw), lambda i, j: (i, j, 0)),
                  pl.BlockSpec((1, POOL_HALO, pw), lambda i, j: (i, jnp.maximum(j * hb - 1, 0), 0)),
                  pl.BlockSpec((1, POOL_HALO, pw), lambda i, j: (i, 0, 0)),
                  _layer_spec(pool_w, e), _layer_spec(scale, e)],
        out_specs=pl.BlockSpec((1, tp, pw), lambda i, j: (i, j, 0)),
        out_shape=jax.ShapeDtypeStruct((b, t, pw), BF16),
        compiler_params=_cparams(("parallel", "parallel")),
        name="pool_mixer",
    )(u3, u3, hist16, pool_w, scale)


def _lam(lq1, lk1, lq2, lk2, lam_init):
    return (jnp.exp(jnp.sum(lq1[...] * lk1[...], axis=-1, keepdims=True))
            - jnp.exp(jnp.sum(lq2[...] * lk2[...], axis=-1, keepdims=True)) + lam_init)


def _stack_maps(q):
    lane = lax.broadcasted_iota(jnp.int32, q.shape, 1)
    zero = jnp.zeros_like(q)
    return jnp.concatenate([jnp.where(lane < DIFF_DH, q, zero), jnp.where(lane >= DIFF_DH, q, zero)], axis=0)


def _softmax_step(s, v, m_prev, l_prev, acc_prev):
    m_new = jnp.maximum(m_prev, jnp.max(s, axis=-1, keepdims=True))
    alpha = jnp.exp(m_prev - m_new)
    p = jnp.exp(s - m_new)
    l_new = alpha * l_prev + jnp.sum(p, axis=-1, keepdims=True)
    acc_new = alpha * acc_prev + _bdot(p, v)
    return m_new, l_new, acc_new


def _diff_finish(l, acc, lam, subg, rows, out_scale):
    o = acc[:rows] / l[:rows] - lam * (acc[rows:] / l[rows:])
    o = o * lax.rsqrt(jnp.mean(o * o, axis=-1, keepdims=True) + NORM_EPS) * subg
    return o * out_scale


def _diffp_kernel(qi_ref, kj_ref, qt_ref, k_ref, vt_ref, bias_ref, lq1, lk1, lq2, lk2, subg_ref, o_ref,
                  q2_ref, m_ref, l_ref, acc_ref, *, tq, hps, lam_init):
    step_id = pl.program_id(2)
    i = qi_ref[step_id]
    j = kj_ref[step_id]

    @pl.when(j == 0)
    def _():
        for hh in range(hps):
            qt = qt_ref[0, hh]
            row = lax.broadcasted_iota(jnp.int32, qt.shape, 0)
            zero = jnp.zeros_like(qt)
            q2_ref[hh, :, :tq] = jnp.where(row < DIFF_DH, qt, zero)
            q2_ref[hh, :, tq:] = jnp.where(row >= DIFF_DH, qt, zero)
        m_ref[...] = jnp.full(m_ref.shape, NEG_INF, F32)
        l_ref[...] = jnp.zeros(l_ref.shape, F32)
        acc_ref[...] = jnp.zeros(acc_ref.shape, F32)

    def step(masked):
        for hh in range(hps):
            s = jnp.dot(k_ref[0, :, hh * LANES:(hh + 1) * LANES], q2_ref[hh],
                        preferred_element_type=F32)
            if masked:
                s = s + bias_ref[...]
            m_prev = m_ref[hh]
            m_new = jnp.maximum(m_prev, jnp.max(s, axis=0, keepdims=True))
            alpha = jnp.exp2(m_prev - m_new)
            p = jnp.exp2(s - m_new)
            l_ref[hh] = alpha * l_ref[hh] + jnp.sum(p, axis=0, keepdims=True)
            acc_ref[hh] = alpha * acc_ref[hh] + jnp.dot(vt_ref[0, hh], p.astype(BF16),
                                                        preferred_element_type=F32)
            m_ref[hh] = m_new

    @pl.when(j < i)
    def _():
        step(False)

    @pl.when(j == i)
    def _():
        step(True)
        lam = _lam(lq1, lk1, lq2, lk2, lam_init)
        for hh in range(hps):
            acc = acc_ref[hh]
            l = l_ref[hh]
            ot = acc[:, :tq] / l[:, :tq] - lam * (acc[:, tq:] / l[:, tq:])
            ot = ot * lax.rsqrt(jnp.mean(ot * ot, axis=0, keepdims=True) + NORM_EPS)
            o_ref[0, :, hh * LANES:(hh + 1) * LANES] = (ot.T * subg_ref[...] * (1.0 - lam_init)).astype(BF16)


def _diff_prompt(qt, kb3, vt, lam_vecs, subg, e, lam_init, tq, hps):
    b, t, w = kb3.shape
    heads = w // LANES
    nq = t // tq
    pairs = [(i, j) for i in range(nq) for j in range(i + 1)]
    qi = jnp.asarray(np.array([p[0] for p in pairs], np.int32))
    kj = jnp.asarray(np.array([p[1] for p in pairs], np.int32))
    key_chunk = np.arange(tq)[:, None] // CHUNK
    query_chunk = (np.arange(2 * tq)[None, :] % tq) // CHUNK
    bias = jnp.asarray(np.where(key_chunk <= query_chunk, 0.0, NEG_INF).astype(np.float32))
    qspec = pl.BlockSpec((1, hps, LANES, tq), lambda bb, h, s, qi_r, kj_r: (bb, h, 0, qi_r[s]))
    kspec = pl.BlockSpec((1, tq, hps * LANES), lambda bb, h, s, qi_r, kj_r: (bb, kj_r[s], h))
    vspec = pl.BlockSpec((1, hps, LANES, tq), lambda bb, h, s, qi_r, kj_r: (bb, h, 0, kj_r[s]))
    vec = lambda a: pl.BlockSpec((None,) + tuple(a.shape[1:]), lambda *_: (e, 0, 0))
    grid_spec = pltpu.PrefetchScalarGridSpec(
        num_scalar_prefetch=2,
        grid=(b, heads // hps, len(pairs)),
        in_specs=[qspec, kspec, vspec,
                  pl.BlockSpec(bias.shape, lambda *_: (0, 0), pipeline_mode=pl.Buffered(1))]
        + [vec(a) for a in lam_vecs] + [vec(subg)],
        out_specs=pl.BlockSpec((1, tq, hps * LANES), lambda bb, h, s, qi_r, kj_r: (bb, qi_r[s], h)),
        scratch_shapes=[pltpu.VMEM((hps, LANES, 2 * tq), BF16), pltpu.VMEM((hps, 1, 2 * tq), F32),
                        pltpu.VMEM((hps, 1, 2 * tq), F32), pltpu.VMEM((hps, LANES, 2 * tq), F32)],
    )
    return pl.pallas_call(
        functools.partial(_diffp_kernel, tq=tq, hps=hps, lam_init=lam_init),
        grid_spec=grid_spec,
        out_shape=jax.ShapeDtypeStruct((b, t, w), BF16),
        compiler_params=_cparams(("parallel", "parallel", "arbitrary")),
        name="diff_attn_prompt",
    )(qi, kj, qt, kb3, vt, bias, *lam_vecs, subg)


def _diffs_kernel(q_ref, ck_ref, cv_ref, kn_ref, vn_ref, lq1, lk1, lq2, lk2, subg_ref, o_ref,
                  m_ref, l_ref, acc_ref, *, ts, heads, lam_init):
    j = pl.program_id(1)
    nj = pl.num_programs(1)
    nq = heads * 2 * ts
    q2 = jnp.concatenate([_stack_maps(q_ref[0, :, h * LANES:(h + 1) * LANES]) for h in range(heads)], axis=0)

    def visible(ncols, valid_cols):
        qh = _div_pow2(lax.broadcasted_iota(jnp.int32, (nq, ncols), 0), 2 * ts)
        col = lax.broadcasted_iota(jnp.int32, (nq, ncols), 1)
        ok = _mod_pow2(col, heads) == qh
        return ok if valid_cols is None else ok & (col < valid_cols)

    @pl.when(j == 0)
    def _():
        nnew = kn_ref.shape[1]
        pad = jnp.zeros((LANES - nnew, LANES), F32)
        kn = jnp.concatenate([kn_ref[0], pad], axis=0)
        vn = jnp.concatenate([vn_ref[0], pad], axis=0)
        s = jnp.where(visible(LANES, nnew), _bdot_nt(q2, kn), NEG_INF)
        m, l, acc = _softmax_step(s, vn, jnp.full((nq, 1), NEG_INF, F32), jnp.zeros((nq, 1), F32),
                                  jnp.zeros((nq, LANES), F32))
        m_ref[...] = m
        l_ref[...] = l
        acc_ref[...] = acc

    s = _bdot_nt(q2, ck_ref[...])
    s = jnp.where(visible(s.shape[1], None), s, NEG_INF)
    m, l, acc = _softmax_step(s, cv_ref[...], m_ref[...], l_ref[...], acc_ref[...])
    m_ref[...] = m
    l_ref[...] = l
    acc_ref[...] = acc

    @pl.when(j == nj - 1)
    def _():
        lam = _lam(lq1, lk1, lq2, lk2, lam_init)
        l_all = l_ref[...]
        acc_all = acc_ref[...]
        for h in range(heads):
            rows = slice(h * 2 * ts, (h + 1) * 2 * ts)
            o_ref[0, :, h * LANES:(h + 1) * LANES] = _diff_finish(
                l_all[rows], acc_all[rows], lam, subg_ref[...], ts, 1.0 - lam_init).astype(BF16)


def _diff_sample(q3, ck5, cv5, kn_rows, vn_rows, lam_vecs, subg, e, lam_init, tk):
    b, ts, w = q3.shape
    nl, _, past, heads, _ = ck5.shape
    assert heads * LANES == w and heads * ts <= LANES
    assert past % CHUNK == 0 and ts <= CHUNK and past % tk == 0
    rows = lambda a: a.reshape(a.shape[:-3] + (a.shape[-3] * heads, LANES))
    qspec = pl.BlockSpec((1, ts, w), lambda bb, j: (bb, 0, 0))
    new = pl.BlockSpec((1, ts * heads, LANES), lambda bb, j: (bb, 0, 0))
    cache = pl.BlockSpec((None, None, tk * heads, LANES), lambda bb, j: (e, bb, j, 0))
    vec = lambda a: pl.BlockSpec((None,) + tuple(a.shape[1:]), lambda *_: (e, 0, 0))
    nq = heads * 2 * ts
    return pl.pallas_call(
        functools.partial(_diffs_kernel, ts=ts, heads=heads, lam_init=lam_init),
        grid=(b, past // tk),
        in_specs=[qspec, cache, cache, new, new] + [vec(a) for a in lam_vecs] + [vec(subg)],
        out_specs=qspec,
        out_shape=jax.ShapeDtypeStruct((b, ts, w), BF16),
        scratch_shapes=[pltpu.VMEM((nq, 1), F32), pltpu.VMEM((nq, 1), F32), pltpu.VMEM((nq, LANES), F32)],
        compiler_params=_cparams(("parallel", "arbitrary")),
        name="diff_attn_sample",
    )(q3, rows(ck5), rows(cv5), kn_rows, vn_rows, *lam_vecs, subg)


def _mixxa_compute(x_ref, m1_ref, m2_ref, w1_ref, w2_ref, g_ref, wq_ref, mk_ref, mv_ref, wo_ref, *, nseq, heads):
    x1 = x_ref[...] + _bdot(m1_ref[...], w1_ref[...]) + _bdot(m2_ref[...], w2_ref[...])
    tm, d = x1.shape
    dh = d // heads
    h = _rms(x1, g_ref[...])
    q = (_bdot(h, wq_ref[...]) * (dh ** -0.5)).astype(BF16)
    rq = tm // nseq
    if len(mk_ref.shape) == 4:
        nrows = mk_ref.shape[1] * heads
        qh = _div_pow2(lax.broadcasted_iota(jnp.int32, (heads * rq, nrows), 0), rq)
        own = _mod_pow2(lax.broadcasted_iota(jnp.int32, (heads * rq, nrows), 1), heads) == qh
        outs = []
        for b in range(nseq):
            qb = q[b * rq:(b + 1) * rq]
            qcat = jnp.concatenate([qb[:, hd * dh:(hd + 1) * dh] for hd in range(heads)], axis=0)
            s = jnp.where(own, _bdot_nt(qcat, mk_ref[b].reshape(nrows, dh)), NEG_INF)
            p = jnp.exp(s - jnp.max(s, axis=-1, keepdims=True))
            p = p / jnp.sum(p, axis=-1, keepdims=True)
            ob = _bdot(p, mv_ref[b].reshape(nrows, dh))
            outs.append(jnp.concatenate([ob[hd * rq:(hd + 1) * rq] for hd in range(heads)], axis=1))
        o = jnp.concatenate(outs, axis=0)
    else:
        q3 = q.reshape(nseq, rq, d)
        outs = []
        for hd in range(heads):
            sl = slice(hd * dh, (hd + 1) * dh)
            s = jnp.einsum("bqd,bkd->bqk", q3[:, :, sl], mk_ref[:, :, sl], preferred_element_type=F32)
            p = jnp.exp(s - jnp.max(s, axis=-1, keepdims=True))
            p = p / jnp.sum(p, axis=-1, keepdims=True)
            outs.append(jnp.einsum("bqk,bkd->bqd", p.astype(BF16), mv_ref[:, :, sl], preferred_element_type=F32))
        o = jnp.concatenate(outs, axis=-1).reshape(tm, d)
    return x1 + _bdot(o, wo_ref[...])


def _mixxa_kernel(*refs, nseq, heads):
    refs[-1][...] = _mixxa_compute(*refs[:-1], nseq=nseq, heads=heads)


def _mixxa_ffn_kernel(*refs, nseq, heads, nchunk, final):
    x2 = _mixxa_compute(*refs[:10], nseq=nseq, heads=heads)
    refs[-1][...] = _ffn_compute(x2, *refs[10:-1], nchunk=nchunk, final=final)


FFN_CHUNKS = 2


def _mix_xa(x2d, m1, m2, w_mix, lmix, g, wq, mk, mv, wo, layer, tm, rows_per_seq, ffn=None):
    n, d = x2d.shape
    half = d // 2
    nseq = max(tm // rows_per_seq, 1)
    tiles_per_seq = max(rows_per_seq // tm, 1)
    row = lambda c: pl.BlockSpec((tm, c), lambda i: (i, 0))
    col = lambda c: pl.BlockSpec((tm, half), lambda i: (i, c))
    tail = (0,) * (mk.ndim - 2)
    mem = pl.BlockSpec((None, nseq) + mk.shape[2:], lambda i: (layer, i // tiles_per_seq) + tail)
    ins = [x2d, m1[0], m2[0], w_mix, w_mix, g, wq, mk, mv, wo]
    specs = [row(d), col(m1[1]), col(m2[1]),
             _layer_spec(w_mix, lmix, (half, d), (0, 0)), _layer_spec(w_mix, lmix, (half, d), (1, 0)),
             _layer_spec(g, layer), _layer_spec(wq, layer), mem, mem, _layer_spec(wo, layer)]
    if ffn is None:
        body = functools.partial(_mixxa_kernel, nseq=nseq, heads=XA_HEADS)
    else:
        gf, wg, wu, wd, gfin, final = ffn
        ins += [gf, wg, wu, wd, gfin]
        specs += [_layer_spec(a, layer) for a in (gf, wg, wu, wd)] + [pl.BlockSpec((1, d), lambda i: (0, 0))]
        body = functools.partial(_mixxa_ffn_kernel, nseq=nseq, heads=XA_HEADS, nchunk=FFN_CHUNKS, final=final)
    return pl.pallas_call(
        body,
        grid=(n // tm,),
        in_specs=specs,
        out_specs=row(d),
        out_shape=jax.ShapeDtypeStruct((n, d), F32),
        compiler_params=_cparams(("parallel",)),
        name="mix_xa" if ffn is None else "mix_xa_ffn",
    )(*ins)


def _ffn_compute(x, g_ref, wg_ref, wu_ref, wd_ref, gf_ref, *, nchunk, final):
    h = _rms(x, g_ref[...]).astype(BF16)
    ff = wg_ref.shape[1]
    fc = ff // nchunk
    acc = x
    for c in range(nchunk):
        sl = slice(c * fc, (c + 1) * fc)
        a = jnp.dot(h, wg_ref[:, sl], preferred_element_type=F32)
        b = jnp.dot(h, wu_ref[:, sl], preferred_element_type=F32)
        acc = acc + _bdot(a * _sigmoid(a) * b, wd_ref[sl, :])
    if final:
        acc = _rms(acc, gf_ref[...])
    return acc


def _ffn_kernel(x_ref, *refs, nchunk, final):
    refs[-1][...] = _ffn_compute(x_ref[...], *refs[:-1], nchunk=nchunk, final=final)


def _ffn(x2d, g, wg, wu, wd, gf, layer, tm, final):
    n, d = x2d.shape
    row = pl.BlockSpec((tm, d), lambda i: (i, 0))
    return pl.pallas_call(
        functools.partial(_ffn_kernel, nchunk=FFN_CHUNKS, final=final),
        grid=(n // tm,),
        in_specs=[row, _layer_spec(g, layer), _layer_spec(wg, layer), _layer_spec(wu, layer),
                  _layer_spec(wd, layer), pl.BlockSpec((1, d), lambda i: (0, 0))],
        out_specs=row,
        out_shape=jax.ShapeDtypeStruct((n, d), F32),
        compiler_params=_cparams(("parallel",)),
        name="ffn",
    )(x2d, g, wg, wu, wd, gf)


_RW_PARAMS = ("mu", "wr", "wk", "wv", "w0", "w1", "w2", "a0", "a1", "a2", "g1", "g2", "k_k", "k_a")


def _pre_rwkv_kernel(*refs, tm, rows_per_seq, hl_rows, has_vres):
    (x_ref, xp_ref, first_ref, g_ref, mu_ref, wr_ref, wk_ref, wv_ref, w0_ref, w1_ref, w2_ref,
     a0_ref, a1_ref, a2_ref, g1_ref, g2_ref, kk_ref, ka_ref) = refs[:18]
    pos = 18
    if has_vres:
        vf_ref, v0_ref, v1_ref, v2_ref = refs[pos:pos + 4]
        pos += 4
    r_ref, k_ref, v_ref, kr_ref, ag_ref, w_ref, gate_ref, hl_ref = refs[pos:]

    i = pl.program_id(0)
    g = g_ref[...]
    h = _rms(x_ref[...], g)
    hp_last = _rms(xp_ref[...], g)[xp_ref.shape[0] - 1:, :]
    rowl = lax.broadcasted_iota(jnp.int32, (tm, 1), 0)
    shifted = jnp.where(rowl == 0, hp_last, pltpu.roll(h, 1, 0))
    is_start = _mod_pow2(i * tm + rowl, rows_per_seq) == 0
    h_prev = jnp.where(is_start, first_ref[...], shifted)
    xx = h_prev - h
    mix = lambda n: (h + xx * mu_ref[n:n + 1, :]).astype(BF16)
    xr, xw, xk, xv, xa, xg = (mix(n) for n in range(6))

    r_ref[...] = _bdot(xr, wr_ref[...]).astype(r_ref.dtype)
    z = w0_ref[...] + _bdot(jnp.tanh(_bdot(xw, w1_ref[...])), w2_ref[...])
    w_ref[...] = -(jnp.maximum(-z, 0.0) + jnp.log1p(jnp.exp(-jnp.abs(z)))) - 0.5
    k = _bdot(xk, wk_ref[...])
    v = _bdot(xv, wv_ref[...])
    if has_vres:
        v = v + (vf_ref[...].astype(F32) - v) * _sigmoid(v0_ref[...] + _bdot(_bdot(xv, v1_ref[...]), v2_ref[...]))
    v_ref[...] = v.astype(v_ref.dtype)
    a = _sigmoid(a0_ref[...] + _bdot(_bdot(xa, a1_ref[...]), a2_ref[...]))
    ag_ref[...] = a.astype(ag_ref.dtype)
    gate_ref[...] = _bdot(_sigmoid(_bdot(xg, g1_ref[...])), g2_ref[...]).astype(gate_ref.dtype)
    kr_ref[...] = (k * kk_ref[...]).astype(kr_ref.dtype)
    k_ref[...] = (k * (1.0 + (a - 1.0) * ka_ref[...])).astype(k_ref.dtype)
    hl_ref[...] = h[tm - hl_rows:, :]


def _pre_rwkv(x2d, first, g, W, layer, o, vf, tm, rows_per_seq, hl_rows):
    n, d = x2d.shape
    has_vres = vf is not None
    xp_rows = 8
    row = pl.BlockSpec((tm, d), lambda i: (i, 0))
    ins = [x2d, x2d, first, g] + [W["rw_" + nm] for nm in _RW_PARAMS]
    specs = [row,
             pl.BlockSpec((xp_rows, d), lambda i: (jnp.maximum(i * (tm // xp_rows) - 1, 0), 0)),
             pl.BlockSpec((tm, d), lambda i: (i // max(rows_per_seq // tm, 1), 0)),
             _layer_spec(g, layer)] + [_layer_spec(W["rw_" + nm], o) for nm in _RW_PARAMS]
    if has_vres:
        vres = [W["rw_v0"], W["rw_v1"], W["rw_v2"]]
        ins += [vf] + vres
        specs += [row] + [_layer_spec(a, o - 1) for a in vres]
    nt = n // tm
    dts = [BF16, BF16, BF16, BF16, BF16, F32, BF16]
    return pl.pallas_call(
        functools.partial(_pre_rwkv_kernel, tm=tm, rows_per_seq=rows_per_seq, hl_rows=hl_rows,
                          has_vres=has_vres),
        grid=(nt,),
        in_specs=specs,
        out_specs=[row] * 7 + [pl.BlockSpec((hl_rows, d), lambda i: (i, 0))],
        out_shape=[jax.ShapeDtypeStruct((n, d), dt) for dt in dts]
        + [jax.ShapeDtypeStruct((nt * hl_rows, d), F32)],
        compiler_params=_cparams(("parallel",)),
        name="pre_rwkv",
    )(*ins)


def _split3(x):
    x1 = x.astype(BF16)
    r1 = x - x1.astype(F32)
    x2 = r1.astype(BF16)
    x3 = (r1 - x2.astype(F32)).astype(BF16)
    return x1, x2, x3


def _head_sum(x, lo_mask):
    s_lo = jnp.sum(jnp.where(lo_mask, x, 0.0), axis=-1, keepdims=True)
    s_hi = jnp.sum(jnp.where(lo_mask, 0.0, x), axis=-1, keepdims=True)
    return jnp.where(lo_mask, s_lo, s_hi)


def _rwkv_rec_kernel(r_ref, k_ref, v_ref, kr_ref, ag_ref, w_ref, gate_ref, lng_ref, lnb_ref, rk_ref, s0_ref,
                     y_ref, sfin_ref, s_scr, *, L, pairs, nb):
    c = pl.program_id(2)
    nc = pl.num_programs(2)
    P = range(nb * pairs)
    bi = [p // pairs for p in P]
    hp = [p % pairs for p in P]

    @pl.when(c == 0)
    def _():
        zero = jnp.zeros((RW_N, RW_N), F32)
        for p in P:
            top = jnp.concatenate([s0_ref[bi[p], 2 * hp[p]], zero], axis=1)
            bot = jnp.concatenate([zero, s0_ref[bi[p], 2 * hp[p] + 1]], axis=1)
            s_scr[p] = jnp.concatenate([top, bot], axis=0)

    L2 = 2 * L
    sub = min(16, L)
    nblk = L // sub
    lane = lax.broadcasted_iota(jnp.int32, (L, LANES), 1)
    lo = lane < RW_N
    row2 = lax.broadcasted_iota(jnp.int32, (L2, L2), 0)
    col2 = lax.broadcasted_iota(jnp.int32, (L2, L2), 1)
    same = _div_pow2(row2, L) == _div_pow2(col2, L)
    strict = same & (col2 < row2)
    incl = same & (col2 <= row2)
    diag_blk = _div_pow2(row2, sub) == _div_pow2(col2, sub)
    eye = (row2 == col2).astype(F32)
    tri = (lax.broadcasted_iota(jnp.int32, (L, L), 1) <= lax.broadcasted_iota(jnp.int32, (L, L), 0)).astype(BF16)
    dot = lambda a, b: jnp.dot(a, b, preferred_element_type=F32)

    def stack(x):
        xb = x.astype(BF16)
        zero = jnp.zeros_like(xb)
        return jnp.concatenate([jnp.where(lo, xb, zero), jnp.where(lo, zero, xb)], axis=0)

    sl = [slice(hp[p] * LANES, (hp[p] + 1) * LANES) for p in P]
    r = [r_ref[bi[p], :, sl[p]].astype(F32) for p in P]
    k = [k_ref[bi[p], :, sl[p]].astype(F32) for p in P]
    v = [v_ref[bi[p], :, sl[p]].astype(F32) for p in P]
    logd = [-jnp.exp(w_ref[bi[p], :, sl[p]]) for p in P]
    kk = []
    for p in P:
        kr = kr_ref[bi[p], :, sl[p]].astype(F32)
        kk.append(kr / jnp.maximum(jnp.sqrt(_head_sum(kr * kr, lo)), 1e-12))
    cum = []
    for p in P:
        l1, l2, l3 = _split3(logd[p])
        cum.append(dot(tri, l1) + dot(tri, l2) + dot(tri, l3))
    ar_s, bk_s, v_s, bkh_s, p_last = [], [], [], [], []
    for p in P:
        e_pos = jnp.exp(cum[p])
        e_neg = jnp.exp(-cum[p])
        pl_ = e_pos[L - 1:, :]
        bt = kk[p] * ag_ref[bi[p], :, sl[p]].astype(F32) * e_neg
        kt = k[p] * e_neg
        ar_s.append(jnp.concatenate([stack(-kk[p] * jnp.exp(cum[p] - logd[p])), stack(r[p] * e_pos)], axis=0))
        bk_s.append(jnp.concatenate([stack(bt), stack(kt)], axis=0))
        bkh_s.append(jnp.concatenate([stack(bt * pl_), stack(kt * pl_)], axis=0))
        v_s.append(stack(v[p]))
        p_last.append(pl_)
    s_bd = [s_scr[p] for p in P]
    ss = [_bdot_nt(ar_s[p], s_bd[p]) for p in P]
    g_all = [_bdot_nt(ar_s[p], bk_s[p]) for p in P]
    a_ab = [jnp.where(strict, g_all[p][:L2, :L2], 0.0) for p in P]
    a_akrk = [jnp.concatenate([jnp.where(strict, g_all[p][:L2, L2:], 0.0),
                               jnp.where(incl, g_all[p][L2:, L2:], 0.0)], axis=0) for p in P]
    a_rb = [jnp.where(incl, g_all[p][L2:, :L2], 0.0) for p in P]
    sv = [_bdot(a_akrk[p], v_s[p]) for p in P]
    wmat = [ss[p][:L2] + sv[p][:L2] for p in P]

    a_d = [jnp.where(diag_blk, a_ab[p], 0.0) for p in P]
    pm = [eye + a_d[p] for p in P]
    pw = a_d
    for _ in range(_log2(sub) - 1):
        pw = [_bdot(pw[p], pw[p]) for p in P]
        pm = [pm[p] + _bdot(pw[p], pm[p]) for p in P]
    if nblk > 1:
        xn = [_bdot(pm[p], jnp.concatenate([wmat[p], a_ab[p] - a_d[p]], axis=1)) for p in P]
        u = [xn[p][:, :LANES] for p in P]
        nm = [xn[p][:, LANES:] for p in P]
        steps = _log2(nblk)
        for it in range(steps):
            u = [u[p] + _bdot(nm[p], u[p]) for p in P]
            if it < steps - 1:
                nm = [_bdot(nm[p], nm[p]) for p in P]
    else:
        u = [_bdot(pm[p], wmat[p]) for p in P]

    y2 = [ss[p][L2:] + sv[p][L2:] + _bdot(a_rb[p], u[p]) for p in P]
    for p in P:
        uv = jnp.concatenate([u[p].astype(BF16), v_s[p]], axis=0)
        s_scr[p] = s_bd[p] * p_last[p] + _bdot_tn(uv, bkh_s[p])
    for p in P:
        y = y2[p][:L] + y2[p][L:]
        mean = _head_sum(y, lo) * (1.0 / RW_N)
        yc = y - mean
        var = _head_sum(yc * yc, lo) * (1.0 / RW_N)
        yn = yc * lax.rsqrt(var + RW_LN_EPS) * lng_ref[:, sl[p]] + lnb_ref[:, sl[p]]
        bonus = _head_sum(r[p] * k[p] * rk_ref[:, sl[p]], lo) * v[p]
        y_ref[bi[p], :, sl[p]] = ((yn + bonus) * gate_ref[bi[p], :, sl[p]].astype(F32)).astype(BF16)

    @pl.when(c == nc - 1)
    def _():
        for p in P:
            s = s_scr[p]
            sfin_ref[bi[p], 2 * hp[p]] = s[:RW_N, :RW_N]
            sfin_ref[bi[p], 2 * hp[p] + 1] = s[RW_N:, RW_N:]


def _rwkv_rec(arrs, lng, lnb, rk, s0, o, L, pairs, nb):
    b, t, d = arrs[0].shape
    heads = d // RW_N
    groups = heads // (2 * pairs)
    w = pairs * LANES
    tok = pl.BlockSpec((nb, L, w), lambda bb, gg, c: (bb, c, gg))
    vec = pl.BlockSpec((None, 1, w), lambda bb, gg, c: (o, 0, gg))
    st_in = pl.BlockSpec((None, nb, 2 * pairs, RW_N, RW_N), lambda bb, gg, c: (o, bb, gg, 0, 0))
    st_out = pl.BlockSpec((nb, 2 * pairs, RW_N, RW_N), lambda bb, gg, c: (bb, gg, 0, 0))
    return pl.pallas_call(
        functools.partial(_rwkv_rec_kernel, L=L, pairs=pairs, nb=nb),
        grid=(b // nb, groups, t // L),
        in_specs=[tok] * 7 + [vec, vec, vec, st_in],
        out_specs=[tok, st_out],
        out_shape=[jax.ShapeDtypeStruct((b, t, d), BF16),
                   jax.ShapeDtypeStruct((b, heads, RW_N, RW_N), F32)],
        scratch_shapes=[pltpu.VMEM((nb * pairs, LANES, LANES), F32)],
        compiler_params=_cparams(("parallel", "parallel", "arbitrary")),
        name="rwkv_rec",
    )(*arrs, lng, lnb, rk, s0)


def _trunk(x, pos0, diff_k_past, diff_v_past, pool_hist, rw_shift, rw_state, mem_k, mem_v, W, cfg):
    b, t, d = x.shape
    n = b * t
    depth = W["norm_mix_g"].shape[0]
    tm, tm_rw, tm_xa = cfg["tm"], cfg["tm_rw"], cfg["tm_xa"]
    x2 = x.reshape(n, d)
    new_k, new_v, new_pool, new_shift, new_s = [], [], [], [], []
    v_first = None
    half = d // 2
    heads = half // LANES
    for l in range(depth):
        if l % 2 == 0:
            e = l // 2
            prompt = diff_k_past is None
            outs = _pre_even(x2, W["norm_mix_g"], W["ev_w_in"], l, e, tm, t, transposed=prompt)
            u, k_rows, v_rows = outs[:3]
            new_k.append(k_rows.reshape(b, t, heads, 2 * DIFF_DH))
            new_v.append(v_rows.reshape(b, t, heads, DIFF_VD))
            u3 = u.reshape(b, t, half)
            new_pool.append(u3[:, t - POOL_HIST:])
            hist16 = jnp.pad(pool_hist[e], ((0, 0), (POOL_HALO - POOL_HIST, 0), (0, 0)))
            m1 = _pool(u3, hist16, W["ev_pool_w"], W["ev_pool_scale"], e, cfg["tp"], pos0).reshape(n, half)
            lam_init = 0.8 - 0.6 * math.exp(-0.3 * l)
            lam_vecs = [W[nm] for nm in ("ev_lam_q1", "ev_lam_k1", "ev_lam_q2", "ev_lam_k2")]
            if prompt:
                kb, qt, vt = outs[3:]
                m2 = _diff_prompt(qt, kb.reshape(b, t, half), vt, lam_vecs, W["ev_subln_g"], e, lam_init,
                                  cfg["tq"], cfg["hps"])
            else:
                m2 = _diff_sample(outs[3].reshape(b, t, half), diff_k_past, diff_v_past,
                                  k_rows.reshape(b, t * heads, LANES), v_rows.reshape(b, t * heads, LANES),
                                  lam_vecs, W["ev_subln_g"], e, lam_init, cfg["tk"])
            m1, m2 = (m1, 0), (m2.reshape(n, half), 0)
            w_mix, lmix = W["ev_w_out"], e
        else:
            o = l // 2
            rows_first = jnp.broadcast_to(rw_shift[o][:, None, :], (b, min(t, tm_rw), d)).reshape(-1, d)
            outs = _pre_rwkv(x2, rows_first, W["norm_mix_g"], W, l, o, None if o == 0 else v_first, tm_rw, t,
                             cfg["hl_rows"])
            r, k, v, kr, ag, wl, gate, hl = outs
            if o == 0:
                v_first = v
            if cfg["hl_rows"] == tm_rw:
                new_shift.append(hl.reshape(b, t, d)[:, -1])
            else:
                per_seq = t // tm_rw
                new_shift.append(hl.reshape(b, per_seq, cfg["hl_rows"], d)[:, -1, -1])
            arrs = [a.reshape(b, t, d) for a in (r, k, v, kr, ag, wl, gate)]
            y, s_fin = _rwkv_rec(arrs, W["rw_lnx_g"], W["rw_lnx_b"], W["rw_r_k"], rw_state, o, cfg["L"],
                                 cfg["pairs"], cfg["nb"])
            new_s.append(s_fin)
            y2 = y.reshape(n, d)
            m1, m2 = (y2, 0), (y2, 1)
            w_mix, lmix = W["rw_wo"], o
        ffn = (W["norm_ffn_g"], W["ffn_wg"], W["ffn_wu"], W["ffn_wd"], W["final_norm_g"], l == depth - 1)
        if cfg["fuse_ffn"]:
            x2 = _mix_xa(x2, m1, m2, w_mix, lmix, W["norm_xa_g"], W["xa_wq"], mem_k, mem_v, W["xa_wo"], l, tm_xa,
                         t, ffn=ffn)
        else:
            x2 = _mix_xa(x2, m1, m2, w_mix, lmix, W["norm_xa_g"], W["xa_wq"], mem_k, mem_v, W["xa_wo"], l, tm_xa,
                         t)
            x2 = _ffn(x2, *ffn[:5], l, tm, final=ffn[5])
    return (x2.reshape(b, t, d), jnp.stack(new_k), jnp.stack(new_v), jnp.stack(new_pool),
            jnp.stack(new_shift), jnp.stack(new_s))


def kernel(x_prompt, x_sample, cache_diff_k, cache_diff_v, state_pool, state_rw_shift, state_rw_wkv, cache_mem_k, cache_mem_v, mem_prompt, norm_mix_g, norm_xa_g, norm_ffn_g, final_norm_g, ev_w_in, ev_pool_w, ev_pool_scale, ev_lam_q1, ev_lam_k1, ev_lam_q2, ev_lam_k2, ev_subln_g, ev_w_out, rw_mu, rw_wr, rw_wk, rw_wv, rw_wo, rw_w0, rw_w1, rw_w2, rw_a0, rw_a1, rw_a2, rw_v0, rw_v1, rw_v2, rw_g1, rw_g2, rw_k_k, rw_k_a, rw_r_k, rw_lnx_g, rw_lnx_b, xa_wq, xa_wk, xa_wv, xa_wo, ffn_wg, ffn_wu, ffn_wd):
    bf = lambda a: a.astype(BF16)
    vec = lambda a: a.reshape(a.shape[0], 1, -1)
    W = dict(
        norm_mix_g=vec(norm_mix_g), norm_xa_g=vec(norm_xa_g), norm_ffn_g=vec(norm_ffn_g),
        final_norm_g=final_norm_g.reshape(1, -1),
        ev_w_in=bf(ev_w_in), ev_pool_w=bf(ev_pool_w), ev_pool_scale=vec(ev_pool_scale),
        ev_lam_q1=vec(ev_lam_q1), ev_lam_k1=vec(ev_lam_k1), ev_lam_q2=vec(ev_lam_q2), ev_lam_k2=vec(ev_lam_k2),
        ev_subln_g=vec(ev_subln_g), ev_w_out=bf(ev_w_out),
        rw_mu=rw_mu, rw_wr=bf(rw_wr), rw_wk=bf(rw_wk), rw_wv=bf(rw_wv), rw_wo=bf(rw_wo),
        rw_w0=vec(rw_w0), rw_w1=bf(rw_w1), rw_w2=bf(rw_w2), rw_a0=vec(rw_a0), rw_a1=bf(rw_a1), rw_a2=bf(rw_a2),
        rw_v0=vec(rw_v0), rw_v1=bf(rw_v1), rw_v2=bf(rw_v2), rw_g1=bf(rw_g1), rw_g2=bf(rw_g2),
        rw_k_k=vec(rw_k_k), rw_k_a=vec(rw_k_a), rw_r_k=vec(rw_r_k),
        rw_lnx_g=vec(rw_lnx_g), rw_lnx_b=vec(rw_lnx_b),
        xa_wq=bf(xa_wq), xa_wo=bf(xa_wo), ffn_wg=bf(ffn_wg), ffn_wu=bf(ffn_wu), ffn_wd=bf(ffn_wd),
    )
    depth, d = norm_mix_g.shape
    bp, tp_len = x_prompt.shape[:2]
    bs, ts = x_sample.shape[:2]
    n_even = ev_w_in.shape[0]
    n_odd = rw_wr.shape[0]
    nmem = mem_prompt.shape[1]

    kv_f32, kv_bf = _mem_kv(mem_prompt.reshape(bp * nmem, d), bf(jnp.concatenate([xa_wk, xa_wv], axis=0)))
    p_mem_k = kv_f32[:depth].reshape(depth, bp, nmem, XA_HEADS, d // XA_HEADS)
    p_mem_v = kv_f32[depth:].reshape(depth, bp, nmem, XA_HEADS, d // XA_HEADS)
    pmk = kv_bf[:depth].reshape(depth, bp, nmem, d)
    pmv = kv_bf[depth:].reshape(depth, bp, nmem, d)

    cfg_p = dict(tm=512, tm_rw=256, tm_xa=512, tp=512, tq=512, hps=4, tk=None, L=64, pairs=8, nb=2, hl_rows=8,
                 fuse_ffn=True)
    zero_pool = jnp.zeros((n_even, bp, POOL_HIST, d // 2), F32)
    zero_shift = jnp.zeros((n_odd, bp, d), F32)
    zero_wkv = jnp.zeros((n_odd, bp, d // RW_N, RW_N, RW_N), F32)
    outs_p = _trunk(x_prompt, 0, None, None, zero_pool, zero_shift, zero_wkv, pmk, pmv, W, cfg_p)

    past = cache_diff_k.shape[2]
    cfg_s = dict(tm=bs * ts, tm_rw=bs * ts, tm_xa=4 * ts, tp=ts, tq=None, hps=None, tk=2048, L=ts, pairs=8, nb=2,
                 hl_rows=bs * ts, fuse_ffn=False)
    outs_s = _trunk(x_sample, past, cache_diff_k, cache_diff_v, state_pool, state_rw_shift, state_rw_wkv,
                    cache_mem_k, cache_mem_v, W, cfg_s)

    y_p, pk, pv, pp, psh, pS = outs_p
    y_s, sk, sv, sp, ssh, sS = outs_s
    return (y_p, y_s, pk, pv, pp, psh, pS, p_mem_k, p_mem_v, sk, sv, sp, ssh, sS)
```

```python
import functools
import math

import jax
import jax.numpy as jnp
import numpy as np
from jax import lax
from jax.experimental import pallas as pl
from jax.experimental.pallas import tpu as pltpu

F32 = jnp.float32
BF16 = jnp.bfloat16

CHUNK = 64
POOL_GROUPS = 4
POOL_WINDOWS = (2, 4, 8, 16)
POOL_HIST = max(POOL_WINDOWS) - 1
POOL_HALO = 16
DIFF_DH = 64
DIFF_VD = 2 * DIFF_DH
RW_N = 64
RW_LN_EPS = 64e-5
XA_HEADS = 4
NORM_EPS = 1e-6
NEG_INF = -1e30
LOG2E = math.log2(math.e)
LANES = 128

VMEM_LIMIT = 56 * 1024 * 1024


def _cparams(sem):
    return pltpu.CompilerParams(dimension_semantics=sem, vmem_limit_bytes=VMEM_LIMIT)


def _layer_spec(arr, layer, block=None, index=None):
    block = tuple(arr.shape[1:]) if block is None else tuple(block)
    index = (0,) * len(block) if index is None else tuple(index)
    return pl.BlockSpec((None,) + block, lambda *_: (layer,) + index, pipeline_mode=pl.Buffered(1))


def _bdot(a, b):
    return jnp.dot(a.astype(BF16), b.astype(BF16), preferred_element_type=F32)


def _bdot_nt(a, b):
    return lax.dot_general(a.astype(BF16), b.astype(BF16), (((1,), (1,)), ((), ())),
                           preferred_element_type=F32)


def _bdot_tn(a, b):
    return lax.dot_general(a.astype(BF16), b.astype(BF16), (((0,), (0,)), ((), ())),
                           preferred_element_type=F32)


def _rms(x, g):
    return x * lax.rsqrt(jnp.mean(x * x, axis=-1, keepdims=True) + NORM_EPS) * g


def _sigmoid(x):
    return 1.0 / (1.0 + jnp.exp(-x))


def _log2(c):
    assert c > 0 and c & (c - 1) == 0, c
    return c.bit_length() - 1


def _div_pow2(x, c):
    return lax.shift_right_arithmetic(x, _log2(c))


def _mod_pow2(x, c):
    assert c & (c - 1) == 0, c
    return x & (c - 1)


def _memkv_kernel(x_ref, w_ref, o_ref, ob_ref):
    y = _bdot(x_ref[...], w_ref[0])
    o_ref[0] = y
    ob_ref[0] = y.astype(BF16)


def _mem_kv(mem2d, w_stack):
    n, d = mem2d.shape
    nw = w_stack.shape[0]
    return pl.pallas_call(
        _memkv_kernel,
        grid=(nw,),
        in_specs=[pl.BlockSpec((n, d), lambda i: (0, 0)),
                  pl.BlockSpec((1, d, d), lambda i: (i, 0, 0))],
        out_specs=[pl.BlockSpec((1, n, d), lambda i: (i, 0, 0)),
                   pl.BlockSpec((1, n, d), lambda i: (i, 0, 0))],
        out_shape=[jax.ShapeDtypeStruct((nw, n, d), F32),
                   jax.ShapeDtypeStruct((nw, n, d), BF16)],
        compiler_params=_cparams(("parallel",)),
        name="mem_kv",
    )(mem2d, w_stack)


def _pre_even_kernel(x_ref, g_ref, w_ref, u_ref, k_ref, v_ref, *rest, pw, dw, transposed):
    h = _rms(x_ref[...], g_ref[...])
    z = _bdot(h, w_ref[...])
    tm = z.shape[0]
    heads = dw // LANES
    u_ref[...] = z[:, :pw]
    q = z[:, pw:pw + dw] * (DIFF_DH ** -0.5)
    k = z[:, pw + dw:pw + 2 * dw]
    v = z[:, pw + 2 * dw:]
    for hd in range(heads):
        sl = slice(hd * LANES, (hd + 1) * LANES)
        k_ref[pl.ds(hd, tm, stride=heads), :] = k[:, sl]
        v_ref[pl.ds(hd, tm, stride=heads), :] = v[:, sl]
    if transposed:
        kb_ref, qt_ref, vt_ref = rest
        kb_ref[...] = k.astype(BF16)
        for hd in range(heads):
            sl = slice(hd * LANES, (hd + 1) * LANES)
            qt_ref[0, hd] = (q[:, sl] * LOG2E).T.astype(BF16)
            vt_ref[0, hd] = v[:, sl].T.astype(BF16)
    else:
        (q_ref,) = rest
        q_ref[...] = q.astype(BF16)


def _pre_even(x2d, g, w_in, layer, e, tm, seq_len, transposed):
    n, d = x2d.shape
    pw = d // 2
    dw = d // 2
    heads = dw // LANES
    row = lambda c: pl.BlockSpec((tm, c), lambda i: (i, 0))
    kv_rows = pl.BlockSpec((tm * heads, LANES), lambda i: (i, 0))
    out_specs = [row(pw), kv_rows, kv_rows]
    out_shape = [jax.ShapeDtypeStruct((n, pw), F32), jax.ShapeDtypeStruct((n * heads, LANES), F32),
                 jax.ShapeDtypeStruct((n * heads, LANES), F32)]
    if transposed:
        tps = seq_len // tm
        tspec = pl.BlockSpec((1, heads, LANES, tm), lambda i: (i // tps, 0, 0, i % tps))
        tshape = jax.ShapeDtypeStruct((n // seq_len, heads, LANES, seq_len), BF16)
        out_specs += [row(dw), tspec, tspec]
        out_shape += [jax.ShapeDtypeStruct((n, dw), BF16), tshape, tshape]
    else:
        out_specs += [row(dw)]
        out_shape += [jax.ShapeDtypeStruct((n, dw), BF16)]
    return pl.pallas_call(
        functools.partial(_pre_even_kernel, pw=pw, dw=dw, transposed=transposed),
        grid=(n // tm,),
        in_specs=[row(d), _layer_spec(g, layer), _layer_spec(w_in, e)],
        out_specs=out_specs,
        out_shape=out_shape,
        compiler_params=_cparams(("parallel",)),
        name="pre_even",
    )(x2d, g, w_in)


def _pool_kernel(u_ref, uprev_ref, hist_ref, w_ref, sc_ref, o_ref, *, tp, pos0, gw):
    t = pl.program_id(1)
    u = u_ref[0]
    prev = jnp.where(t == 0, hist_ref[0], uprev_ref[0])
    full = jnp.concatenate([prev, u], axis=0)
    sums = []
    s = full
    for sh in (1, 2, 4, 8):
        s = s + pltpu.roll(s, sh, 0)
        sums.append(s)
    pos = pos0 + t * tp + lax.broadcasted_iota(jnp.int32, (tp, 1), 0)
    for g, w in enumerate(POOL_WINDOWS):
        sl = slice(g * gw, (g + 1) * gw)
        cnt = jnp.minimum(pos + 1, w).astype(F32)
        pooled = sums[g][POOL_HALO:, sl] / cnt - u[:, sl]
        y = _bdot(pooled, w_ref[g]) * sc_ref[:, sl]
        o_ref[0, :, sl] = y.astype(BF16)


def _pool(u3, hist16, pool_w, scale, e, tp, pos0):
    b, t, pw = u3.shape
    gw = pw // POOL_GROUPS
    hb = tp // POOL_HALO
    return pl.pallas_call(
        functools.partial(_pool_kernel, tp=tp, pos0=pos0, gw=gw),
        grid=(b, t // tp),
        in_specs=[pl.BlockSpec((1, tp, pw), lambda i, j: (i, j, 0)),
                  pl.BlockSpec((1, POOL_HALO, pw), lambda i, j: (i, jnp.maximum(j * hb - 1, 0), 0)),
                  pl.BlockSpec((1, POOL_HALO, pw), lambda i, j: (i, 0, 0)),
                  _layer_spec(pool_w, e), _layer_spec(scale, e)],
        out_specs=pl.BlockSpec((1, tp, pw), lambda i, j: (i, j, 0)),
        out_shape=jax.ShapeDtypeStruct((b, t, pw), BF16),
        compiler_params=_cparams(("parallel", "parallel")),
        name="pool_mixer",
    )(u3, u3, hist16, pool_w, scale)


def _lam(lq1, lk1, lq2, lk2, lam_init):
    return (jnp.exp(jnp.sum(lq1[...] * lk1[...], axis=-1, keepdims=True))
            - jnp.exp(jnp.sum(lq2[...] * lk2[...], axis=-1, keepdims=True)) + lam_init)


def _stack_maps(q):
    lane = lax.broadcasted_iota(jnp.int32, q.shape, 1)
    zero = jnp.zeros_like(q)
    return jnp.concatenate([jnp.where(lane < DIFF_DH, q, zero), jnp.where(lane >= DIFF_DH, q, zero)], axis=0)


def _softmax_step(s, v, m_prev, l_prev, acc_prev):
    m_new = jnp.maximum(m_prev, jnp.max(s, axis=-1, keepdims=True))
    alpha = jnp.exp(m_prev - m_new)
    p = jnp.exp(s - m_new)
    l_new = alpha * l_prev + jnp.sum(p, axis=-1, keepdims=True)
    acc_new = alpha * acc_prev + _bdot(p, v)
    return m_new, l_new, acc_new


def _diff_finish(l, acc, lam, subg, rows, out_scale):
    o = acc[:rows] / l[:rows] - lam * (acc[rows:] / l[rows:])
    o = o * lax.rsqrt(jnp.mean(o * o, axis=-1, keepdims=True) + NORM_EPS) * subg
    return o * out_scale


def _diffp_kernel(qi_ref, kj_ref, qt_ref, k_ref, vt_ref, bias_ref, lq1, lk1, lq2, lk2, subg_ref, o_ref,
                  q2_ref, m_ref, l_ref, acc_ref, *, tq, hps, lam_init):
    step_id = pl.program_id(2)
    i = qi_ref[step_id]
    j = kj_ref[step_id]

    @pl.when(j == 0)
    def _():
        for hh in range(hps):
            qt = qt_ref[0, hh]
            row = lax.broadcasted_iota(jnp.int32, qt.shape, 0)
            zero = jnp.zeros_like(qt)
            q2_ref[hh, :, :tq] = jnp.where(row < DIFF_DH, qt, zero)
            q2_ref[hh, :, tq:] = jnp.where(row >= DIFF_DH, qt, zero)
        m_ref[...] = jnp.full(m_ref.shape, NEG_INF, F32)
        l_ref[...] = jnp.zeros(l_ref.shape, F32)
        acc_ref[...] = jnp.zeros(acc_ref.shape, F32)

    def step(masked):
        def scores(hh):
            s = jnp.dot(k_ref[0, :, hh * LANES:(hh + 1) * LANES], q2_ref[hh],
                        preferred_element_type=F32)
            return s + bias_ref[...] if masked else s

        def softmax(hh, s):
            m_prev = m_ref[hh]
            m_new = jnp.maximum(m_prev, jnp.max(s, axis=0, keepdims=True))
            alpha = jnp.exp2(m_prev - m_new)
            p = jnp.exp2(s - m_new)
            l_ref[hh] = alpha * l_ref[hh] + jnp.sum(p, axis=0, keepdims=True)
            m_ref[hh] = m_new
            return alpha, p.astype(BF16)

        def values(hh, alpha, p):
            acc_ref[hh] = alpha * acc_ref[hh] + jnp.dot(vt_ref[0, hh], p, preferred_element_type=F32)

        s_next = scores(0)
        pending = None
        for hh in range(hps):
            s_cur = s_next
            if hh + 1 < hps:
                s_next = scores(hh + 1)
            cur = softmax(hh, s_cur)
            if pending is not None:
                values(hh - 1, *pending)
            pending = cur
        values(hps - 1, *pending)

    @pl.when(j < i)
    def _():
        step(False)

    @pl.when(j == i)
    def _():
        step(True)
        lam = _lam(lq1, lk1, lq2, lk2, lam_init)
        for hh in range(hps):
            acc = acc_ref[hh]
            l = l_ref[hh]
            ot = acc[:, :tq] / l[:, :tq] - lam * (acc[:, tq:] / l[:, tq:])
            ot = ot * lax.rsqrt(jnp.mean(ot * ot, axis=0, keepdims=True) + NORM_EPS)
            o_ref[0, :, hh * LANES:(hh + 1) * LANES] = (ot.T * subg_ref[...] * (1.0 - lam_init)).astype(BF16)


def _diff_prompt(qt, kb3, vt, lam_vecs, subg, e, lam_init, tq, hps):
    b, t, w = kb3.shape
    heads = w // LANES
    nq = t // tq
    pairs = [(i, j) for i in range(nq) for j in range(i + 1)]
    qi = jnp.asarray(np.array([p[0] for p in pairs], np.int32))
    kj = jnp.asarray(np.array([p[1] for p in pairs], np.int32))
    key_chunk = np.arange(tq)[:, None] // CHUNK
    query_chunk = (np.arange(2 * tq)[None, :] % tq) // CHUNK
    bias = jnp.asarray(np.where(key_chunk <= query_chunk, 0.0, NEG_INF).astype(np.float32))
    qspec = pl.BlockSpec((1, hps, LANES, tq), lambda bb, h, s, qi_r, kj_r: (bb, h, 0, qi_r[s]))
    kspec = pl.BlockSpec((1, tq, hps * LANES), lambda bb, h, s, qi_r, kj_r: (bb, kj_r[s], h))
    vspec = pl.BlockSpec((1, hps, LANES, tq), lambda bb, h, s, qi_r, kj_r: (bb, h, 0, kj_r[s]))
    vec = lambda a: pl.BlockSpec((None,) + tuple(a.shape[1:]), lambda *_: (e, 0, 0))
    grid_spec = pltpu.PrefetchScalarGridSpec(
        num_scalar_prefetch=2,
        grid=(b, heads // hps, len(pairs)),
        in_specs=[qspec, kspec, vspec,
                  pl.BlockSpec(bias.shape, lambda *_: (0, 0), pipeline_mode=pl.Buffered(1))]
        + [vec(a) for a in lam_vecs] + [vec(subg)],
        out_specs=pl.BlockSpec((1, tq, hps * LANES), lambda bb, h, s, qi_r, kj_r: (bb, qi_r[s], h)),
        scratch_shapes=[pltpu.VMEM((hps, LANES, 2 * tq), BF16), pltpu.VMEM((hps, 1, 2 * tq), F32),
                        pltpu.VMEM((hps, 1, 2 * tq), F32), pltpu.VMEM((hps, LANES, 2 * tq), F32)],
    )
    return pl.pallas_call(
        functools.partial(_diffp_kernel, tq=tq, hps=hps, lam_init=lam_init),
        grid_spec=grid_spec,
        out_shape=jax.ShapeDtypeStruct((b, t, w), BF16),
        compiler_params=_cparams(("parallel", "parallel", "arbitrary")),
        name="diff_attn_prompt",
    )(qi, kj, qt, kb3, vt, bias, *lam_vecs, subg)


def _diffs_kernel(q_ref, ck_ref, cv_ref, kn_ref, vn_ref, lq1, lk1, lq2, lk2, subg_ref, o_ref,
                  m_ref, l_ref, acc_ref, *, ts, heads, lam_init):
    j = pl.program_id(1)
    nj = pl.num_programs(1)
    nq = heads * 2 * ts
    q2 = jnp.concatenate([_stack_maps(q_ref[0, :, h * LANES:(h + 1) * LANES]) for h in range(heads)], axis=0)

    def visible(ncols, valid_cols):
        qh = _div_pow2(lax.broadcasted_iota(jnp.int32, (nq, ncols), 0), 2 * ts)
        col = lax.broadcasted_iota(jnp.int32, (nq, ncols), 1)
        ok = _mod_pow2(col, heads) == qh
        return ok if valid_cols is None else ok & (col < valid_cols)

    @pl.when(j == 0)
    def _():
        nnew = kn_ref.shape[1]
        pad = jnp.zeros((LANES - nnew, LANES), F32)
        kn = jnp.concatenate([kn_ref[0], pad], axis=0)
        vn = jnp.concatenate([vn_ref[0], pad], axis=0)
        s = jnp.where(visible(LANES, nnew), _bdot_nt(q2, kn), NEG_INF)
        m, l, acc = _softmax_step(s, vn, jnp.full((nq, 1), NEG_INF, F32), jnp.zeros((nq, 1), F32),
                                  jnp.zeros((nq, LANES), F32))
        m_ref[...] = m
        l_ref[...] = l
        acc_ref[...] = acc

    s = _bdot_nt(q2, ck_ref[...])
    s = jnp.where(visible(s.shape[1], None), s, NEG_INF)
    m, l, acc = _softmax_step(s, cv_ref[...], m_ref[...], l_ref[...], acc_ref[...])
    m_ref[...] = m
    l_ref[...] = l
    acc_ref[...] = acc

    @pl.when(j == nj - 1)
    def _():
        lam = _lam(lq1, lk1, lq2, lk2, lam_init)
        l_all = l_ref[...]
        acc_all = acc_ref[...]
        for h in range(heads):
            rows = slice(h * 2 * ts, (h + 1) * 2 * ts)
            o_ref[0, :, h * LANES:(h + 1) * LANES] = _diff_finish(
                l_all[rows], acc_all[rows], lam, subg_ref[...], ts, 1.0 - lam_init).astype(BF16)


def _diff_sample(q3, ck5, cv5, kn_rows, vn_rows, lam_vecs, subg, e, lam_init, tk):
    b, ts, w = q3.shape
    nl, _, past, heads, _ = ck5.shape
    assert heads * LANES == w and heads * ts <= LANES
    assert past % CHUNK == 0 and ts <= CHUNK and past % tk == 0
    rows = lambda a: a.reshape(a.shape[:-3] + (a.shape[-3] * heads, LANES))
    qspec = pl.BlockSpec((1, ts, w), lambda bb, j: (bb, 0, 0))
    new = pl.BlockSpec((1, ts * heads, LANES), lambda bb, j: (bb, 0, 0))
    cache = pl.BlockSpec((None, None, tk * heads, LANES), lambda bb, j: (e, bb, j, 0))
    vec = lambda a: pl.BlockSpec((None,) + tuple(a.shape[1:]), lambda *_: (e, 0, 0))
    nq = heads * 2 * ts
    return pl.pallas_call(
        functools.partial(_diffs_kernel, ts=ts, heads=heads, lam_init=lam_init),
        grid=(b, past // tk),
        in_specs=[qspec, cache, cache, new, new] + [vec(a) for a in lam_vecs] + [vec(subg)],
        out_specs=qspec,
        out_shape=jax.ShapeDtypeStruct((b, ts, w), BF16),
        scratch_shapes=[pltpu.VMEM((nq, 1), F32), pltpu.VMEM((nq, 1), F32), pltpu.VMEM((nq, LANES), F32)],
        compiler_params=_cparams(("parallel", "arbitrary")),
        name="diff_attn_sample",
    )(q3, rows(ck5), rows(cv5), kn_rows, vn_rows, *lam_vecs, subg)


def _mixxa_compute(x_ref, m1_ref, m2_ref, w1_ref, w2_ref, g_ref, wq_ref, mk_ref, mv_ref, wo_ref, *, nseq, heads):
    x1 = x_ref[...] + _bdot(m1_ref[...], w1_ref[...]) + _bdot(m2_ref[...], w2_ref[...])
    tm, d = x1.shape
    dh = d // heads
    h = _rms(x1, g_ref[...])
    q = (_bdot(h, wq_ref[...]) * (dh ** -0.5)).astype(BF16)
    rq = tm // nseq
    if len(mk_ref.shape) == 4:
        nrows = mk_ref.shape[1] * heads
        qh = _div_pow2(lax.broadcasted_iota(jnp.int32, (heads * rq, nrows), 0), rq)
        own = _mod_pow2(lax.broadcasted_iota(jnp.int32, (heads * rq, nrows), 1), heads) == qh
        outs = []
        for b in range(nseq):
            qb = q[b * rq:(b + 1) * rq]
            qcat = jnp.concatenate([qb[:, hd * dh:(hd + 1) * dh] for hd in range(heads)], axis=0)
            s = jnp.where(own, _bdot_nt(qcat, mk_ref[b].reshape(nrows, dh)), NEG_INF)
            p = jnp.exp(s - jnp.max(s, axis=-1, keepdims=True))
            p = p / jnp.sum(p, axis=-1, keepdims=True)
            ob = _bdot(p, mv_ref[b].reshape(nrows, dh))
            outs.append(jnp.concatenate([ob[hd * rq:(hd + 1) * rq] for hd in range(heads)], axis=1))
        o = jnp.concatenate(outs, axis=0)
    else:
        q3 = q.reshape(nseq, rq, d)
        outs = []
        for hd in range(heads):
            sl = slice(hd * dh, (hd + 1) * dh)
            s = jnp.einsum("bqd,bkd->bqk", q3[:, :, sl], mk_ref[:, :, sl], preferred_element_type=F32)
            p = jnp.exp(s - jnp.max(s, axis=-1, keepdims=True))
            p = p / jnp.sum(p, axis=-1, keepdims=True)
            outs.append(jnp.einsum("bqk,bkd->bqd", p.astype(BF16), mv_ref[:, :, sl], preferred_element_type=F32))
        o = jnp.concatenate(outs, axis=-1).reshape(tm, d)
    return x1 + _bdot(o, wo_ref[...])


def _mixxa_kernel(*refs, nseq, heads):
    refs[-1][...] = _mixxa_compute(*refs[:-1], nseq=nseq, heads=heads)


def _mixxa_ffn_kernel(*refs, nseq, heads, nchunk, final):
    x2 = _mixxa_compute(*refs[:10], nseq=nseq, heads=heads)
    refs[-1][...] = _ffn_compute(x2, *refs[10:-1], nchunk=nchunk, final=final)


FFN_CHUNKS = 11


def _mix_xa(x2d, m1, m2, w_mix, lmix, g, wq, mk, mv, wo, layer, tm, rows_per_seq, ffn=None):
    n, d = x2d.shape
    half = d // 2
    nseq = max(tm // rows_per_seq, 1)
    tiles_per_seq = max(rows_per_seq // tm, 1)
    row = lambda c: pl.BlockSpec((tm, c), lambda i: (i, 0))
    col = lambda c: pl.BlockSpec((tm, half), lambda i: (i, c))
    tail = (0,) * (mk.ndim - 2)
    mem = pl.BlockSpec((None, nseq) + mk.shape[2:], lambda i: (layer, i // tiles_per_seq) + tail)
    ins = [x2d, m1[0], m2[0], w_mix, w_mix, g, wq, mk, mv, wo]
    specs = [row(d), col(m1[1]), col(m2[1]),
             _layer_spec(w_mix, lmix, (half, d), (0, 0)), _layer_spec(w_mix, lmix, (half, d), (1, 0)),
             _layer_spec(g, layer), _layer_spec(wq, layer), mem, mem, _layer_spec(wo, layer)]
    if ffn is None:
        body = functools.partial(_mixxa_kernel, nseq=nseq, heads=XA_HEADS)
    else:
        gf, wg, wu, wd, gfin, final = ffn
        ins += [gf, wg, wu, wd, gfin]
        specs += [_layer_spec(a, layer) for a in (gf, wg, wu, wd)] + [pl.BlockSpec((1, d), lambda i: (0, 0))]
        body = functools.partial(_mixxa_ffn_kernel, nseq=nseq, heads=XA_HEADS, nchunk=FFN_CHUNKS, final=final)
    return pl.pallas_call(
        body,
        grid=(n // tm,),
        in_specs=specs,
        out_specs=row(d),
        out_shape=jax.ShapeDtypeStruct((n, d), F32),
        compiler_params=_cparams(("parallel",)),
        name="mix_xa" if ffn is None else "mix_xa_ffn",
    )(*ins)


def _ffn_compute(x, g_ref, wg_ref, wu_ref, wd_ref, gf_ref, *, nchunk, final):
    h = _rms(x, g_ref[...]).astype(BF16)
    ff = wg_ref.shape[1]
    fc = ff // nchunk
    acc = x
    for c in range(nchunk):
        sl = slice(c * fc, (c + 1) * fc)
        a = jnp.dot(h, wg_ref[:, sl], preferred_element_type=F32)
        b = jnp.dot(h, wu_ref[:, sl], preferred_element_type=F32)
        acc = acc + _bdot(a * _sigmoid(a) * b, wd_ref[sl, :])
    if final:
        acc = _rms(acc, gf_ref[...])
    return acc


def _ffn_kernel(x_ref, *refs, nchunk, final):
    refs[-1][...] = _ffn_compute(x_ref[...], *refs[:-1], nchunk=nchunk, final=final)


def _ffn(x2d, g, wg, wu, wd, gf, layer, tm, final):
    n, d = x2d.shape
    row = pl.BlockSpec((tm, d), lambda i: (i, 0))
    return pl.pallas_call(
        functools.partial(_ffn_kernel, nchunk=FFN_CHUNKS, final=final),
        grid=(n // tm,),
        in_specs=[row, _layer_spec(g, layer), _layer_spec(wg, layer), _layer_spec(wu, layer),
                  _layer_spec(wd, layer), pl.BlockSpec((1, d), lambda i: (0, 0))],
        out_specs=row,
        out_shape=jax.ShapeDtypeStruct((n, d), F32),
        compiler_params=_cparams(("parallel",)),
        name="ffn",
    )(x2d, g, wg, wu, wd, gf)


_RW_PARAMS = ("mu", "wr", "wk", "wv", "w0", "w1", "w2", "a0", "a1", "a2", "g1", "g2", "k_k", "k_a")


def _pre_rwkv_kernel(*refs, tm, rows_per_seq, hl_rows, has_vres):
    (x_ref, xp_ref, first_ref, g_ref, mu_ref, wr_ref, wk_ref, wv_ref, w0_ref, w1_ref, w2_ref,
     a0_ref, a1_ref, a2_ref, g1_ref, g2_ref, kk_ref, ka_ref) = refs[:18]
    pos = 18
    if has_vres:
        vf_ref, v0_ref, v1_ref, v2_ref = refs[pos:pos + 4]
        pos += 4
    r_ref, k_ref, v_ref, kr_ref, ag_ref, w_ref, gate_ref, hl_ref = refs[pos:]

    i = pl.program_id(0)
    g = g_ref[...]
    h = _rms(x_ref[...], g)
    hp_last = _rms(xp_ref[...], g)[xp_ref.shape[0] - 1:, :]
    rowl = lax.broadcasted_iota(jnp.int32, (tm, 1), 0)
    shifted = jnp.where(rowl == 0, hp_last, pltpu.roll(h, 1, 0))
    is_start = _mod_pow2(i * tm + rowl, rows_per_seq) == 0
    h_prev = jnp.where(is_start, first_ref[...], shifted)
    xx = h_prev - h
    mix = lambda n: (h + xx * mu_ref[n:n + 1, :]).astype(BF16)
    t_w = _bdot(mix(1), w1_ref[...])
    t_a = _bdot(mix(4), a1_ref[...])
    t_g = _bdot(mix(5), g1_ref[...])
    xv = mix(3)
    if has_vres:
        t_v = _bdot(xv, v1_ref[...])
    r_ref[...] = _bdot(mix(0), wr_ref[...]).astype(r_ref.dtype)
    z = w0_ref[...] + _bdot(jnp.tanh(t_w), w2_ref[...])
    w_ref[...] = -(jnp.maximum(-z, 0.0) + jnp.log1p(jnp.exp(-jnp.abs(z)))) - 0.5
    k = _bdot(mix(2), wk_ref[...])
    a = _sigmoid(a0_ref[...] + _bdot(t_a, a2_ref[...]))
    ag_ref[...] = a.astype(ag_ref.dtype)
    v = _bdot(xv, wv_ref[...])
    if has_vres:
        v = v + (vf_ref[...].astype(F32) - v) * _sigmoid(v0_ref[...] + _bdot(t_v, v2_ref[...]))
    v_ref[...] = v.astype(v_ref.dtype)
    gate_ref[...] = _bdot(_sigmoid(t_g), g2_ref[...]).astype(gate_ref.dtype)
    kr_ref[...] = (k * kk_ref[...]).astype(kr_ref.dtype)
    k_ref[...] = (k * (1.0 + (a - 1.0) * ka_ref[...])).astype(k_ref.dtype)
    hl_ref[...] = h[tm - hl_rows:, :]


def _pre_rwkv(x2d, first, g, W, layer, o, vf, tm, rows_per_seq, hl_rows):
    n, d = x2d.shape
    has_vres = vf is not None
    xp_rows = 8
    row = pl.BlockSpec((tm, d), lambda i: (i, 0))
    ins = [x2d, x2d, first, g] + [W["rw_" + nm] for nm in _RW_PARAMS]
    specs = [row,
             pl.BlockSpec((xp_rows, d), lambda i: (jnp.maximum(i * (tm // xp_rows) - 1, 0), 0)),
             pl.BlockSpec((tm, d), lambda i: (i // max(rows_per_seq // tm, 1), 0)),
             _layer_spec(g, layer)] + [_layer_spec(W["rw_" + nm], o) for nm in _RW_PARAMS]
    if has_vres:
        vres = [W["rw_v0"], W["rw_v1"], W["rw_v2"]]
        ins += [vf] + vres
        specs += [row] + [_layer_spec(a, o - 1) for a in vres]
    nt = n // tm
    dts = [BF16, BF16, BF16, BF16, BF16, F32, BF16]
    return pl.pallas_call(
        functools.partial(_pre_rwkv_kernel, tm=tm, rows_per_seq=rows_per_seq, hl_rows=hl_rows,
                          has_vres=has_vres),
        grid=(nt,),
        in_specs=specs,
        out_specs=[row] * 7 + [pl.BlockSpec((hl_rows, d), lambda i: (i, 0))],
        out_shape=[jax.ShapeDtypeStruct((n, d), dt) for dt in dts]
        + [jax.ShapeDtypeStruct((nt * hl_rows, d), F32)],
        compiler_params=_cparams(("parallel",)),
        name="pre_rwkv",
    )(*ins)


def _split3(x):
    x1 = x.astype(BF16)
    r1 = x - x1.astype(F32)
    x2 = r1.astype(BF16)
    x3 = (r1 - x2.astype(F32)).astype(BF16)
    return x1, x2, x3


def _head_sum(x, lo_mask):
    s_lo = jnp.sum(jnp.where(lo_mask, x, 0.0), axis=-1, keepdims=True)
    s_hi = jnp.sum(jnp.where(lo_mask, 0.0, x), axis=-1, keepdims=True)
    return jnp.where(lo_mask, s_lo, s_hi)


def _rwkv_rec_kernel(r_ref, k_ref, v_ref, kr_ref, ag_ref, w_ref, gate_ref, lng_ref, lnb_ref, rk_ref, s0_ref,
                     y_ref, sfin_ref, s_scr, *, L, pairs, nb):
    c = pl.program_id(2)
    nc = pl.num_programs(2)
    P = range(nb * pairs)
    bi = [p // pairs for p in P]
    hp = [p % pairs for p in P]

    @pl.when(c == 0)
    def _():
        zero = jnp.zeros((RW_N, RW_N), F32)
        for p in P:
            top = jnp.concatenate([s0_ref[bi[p], 2 * hp[p]], zero], axis=1)
            bot = jnp.concatenate([zero, s0_ref[bi[p], 2 * hp[p] + 1]], axis=1)
            s_scr[p] = jnp.concatenate([top, bot], axis=0)

    L2 = 2 * L
    sub = min(16, L)
    nblk = L // sub
    lane = lax.broadcasted_iota(jnp.int32, (L, LANES), 1)
    lo = lane < RW_N
    row2 = lax.broadcasted_iota(jnp.int32, (L2, L2), 0)
    col2 = lax.broadcasted_iota(jnp.int32, (L2, L2), 1)
    same = _div_pow2(row2, L) == _div_pow2(col2, L)
    strict = same & (col2 < row2)
    incl = same & (col2 <= row2)
    diag_blk = _div_pow2(row2, sub) == _div_pow2(col2, sub)
    eye = (row2 == col2).astype(F32)
    tri = (lax.broadcasted_iota(jnp.int32, (L, L), 1) <= lax.broadcasted_iota(jnp.int32, (L, L), 0)).astype(BF16)
    dot = lambda a, b: jnp.dot(a, b, preferred_element_type=F32)

    def stack(x):
        xb = x.astype(BF16)
        zero = jnp.zeros_like(xb)
        return jnp.concatenate([jnp.where(lo, xb, zero), jnp.where(lo, zero, xb)], axis=0)

    sl = [slice(hp[p] * LANES, (hp[p] + 1) * LANES) for p in P]
    r = [r_ref[bi[p], :, sl[p]].astype(F32) for p in P]
    k = [k_ref[bi[p], :, sl[p]].astype(F32) for p in P]
    v = [v_ref[bi[p], :, sl[p]].astype(F32) for p in P]
    logd = [-jnp.exp(w_ref[bi[p], :, sl[p]]) for p in P]
    kk = []
    for p in P:
        kr = kr_ref[bi[p], :, sl[p]].astype(F32)
        kk.append(kr / jnp.maximum(jnp.sqrt(_head_sum(kr * kr, lo)), 1e-12))
    cum = []
    for p in P:
        l1, l2, l3 = _split3(logd[p])
        cum.append(dot(tri, l1) + dot(tri, l2) + dot(tri, l3))
    ar_s, bk_s, v_s, bkh_s, p_last = [], [], [], [], []
    for p in P:
        e_pos = jnp.exp(cum[p])
        e_neg = jnp.exp(-cum[p])
        pl_ = e_pos[L - 1:, :]
        bt = kk[p] * ag_ref[bi[p], :, sl[p]].astype(F32) * e_neg
        kt = k[p] * e_neg
        ar_s.append(jnp.concatenate([stack(-kk[p] * jnp.exp(cum[p] - logd[p])), stack(r[p] * e_pos)], axis=0))
        bk_s.append(jnp.concatenate([stack(bt), stack(kt)], axis=0))
        bkh_s.append(jnp.concatenate([stack(bt * pl_), stack(kt * pl_)], axis=0))
        v_s.append(stack(v[p]))
        p_last.append(pl_)
    s_bd = [s_scr[p] for p in P]
    ss = [_bdot_nt(ar_s[p], s_bd[p]) for p in P]
    g_all = [_bdot_nt(ar_s[p], bk_s[p]) for p in P]
    a_ab = [jnp.where(strict, g_all[p][:L2, :L2], 0.0) for p in P]
    a_akrk = [jnp.concatenate([jnp.where(strict, g_all[p][:L2, L2:], 0.0),
                               jnp.where(incl, g_all[p][L2:, L2:], 0.0)], axis=0) for p in P]
    a_rb = [jnp.where(incl, g_all[p][L2:, :L2], 0.0) for p in P]
    sv = [_bdot(a_akrk[p], v_s[p]) for p in P]
    wmat = [ss[p][:L2] + sv[p][:L2] for p in P]

    a_d = [jnp.where(diag_blk, a_ab[p], 0.0) for p in P]
    pm = [eye + a_d[p] for p in P]
    pw = a_d
    for _ in range(_log2(sub) - 1):
        pw = [_bdot(pw[p], pw[p]) for p in P]
        pm = [pm[p] + _bdot(pw[p], pm[p]) for p in P]
    if nblk > 1:
        xn = [_bdot(pm[p], jnp.concatenate([wmat[p], a_ab[p] - a_d[p]], axis=1)) for p in P]
        u = [xn[p][:, :LANES] for p in P]
        nm = [xn[p][:, LANES:] for p in P]
        steps = _log2(nblk)
        for it in range(steps):
            u = [u[p] + _bdot(nm[p], u[p]) for p in P]
            if it < steps - 1:
                nm = [_bdot(nm[p], nm[p]) for p in P]
    else:
        u = [_bdot(pm[p], wmat[p]) for p in P]

    y2 = [ss[p][L2:] + sv[p][L2:] + _bdot(a_rb[p], u[p]) for p in P]
    for p in P:
        uv = jnp.concatenate([u[p].astype(BF16), v_s[p]], axis=0)
        s_scr[p] = s_bd[p] * p_last[p] + _bdot_tn(uv, bkh_s[p])
    for p in P:
        y = y2[p][:L] + y2[p][L:]
        mean = _head_sum(y, lo) * (1.0 / RW_N)
        yc = y - mean
        var = _head_sum(yc * yc, lo) * (1.0 / RW_N)
        yn = yc * lax.rsqrt(var + RW_LN_EPS) * lng_ref[:, sl[p]] + lnb_ref[:, sl[p]]
        bonus = _head_sum(r[p] * k[p] * rk_ref[:, sl[p]], lo) * v[p]
        y_ref[bi[p], :, sl[p]] = ((yn + bonus) * gate_ref[bi[p], :, sl[p]].astype(F32)).astype(BF16)

    @pl.when(c == nc - 1)
    def _():
        for p in P:
            s = s_scr[p]
            sfin_ref[bi[p], 2 * hp[p]] = s[:RW_N, :RW_N]
            sfin_ref[bi[p], 2 * hp[p] + 1] = s[RW_N:, RW_N:]


def _rwkv_rec(arrs, lng, lnb, rk, s0, o, L, pairs, nb):
    b, t, d = arrs[0].shape
    heads = d // RW_N
    groups = heads // (2 * pairs)
    w = pairs * LANES
    tok = pl.BlockSpec((nb, L, w), lambda bb, gg, c: (bb, c, gg))
    vec = pl.BlockSpec((None, 1, w), lambda bb, gg, c: (o, 0, gg))
    st_in = pl.BlockSpec((None, nb, 2 * pairs, RW_N, RW_N), lambda bb, gg, c: (o, bb, gg, 0, 0))
    st_out = pl.BlockSpec((nb, 2 * pairs, RW_N, RW_N), lambda bb, gg, c: (bb, gg, 0, 0))
    return pl.pallas_call(
        functools.partial(_rwkv_rec_kernel, L=L, pairs=pairs, nb=nb),
        grid=(b // nb, groups, t // L),
        in_specs=[tok] * 7 + [vec, vec, vec, st_in],
        out_specs=[tok, st_out],
        out_shape=[jax.ShapeDtypeStruct((b, t, d), BF16),
                   jax.ShapeDtypeStruct((b, heads, RW_N, RW_N), F32)],
        scratch_shapes=[pltpu.VMEM((nb * pairs, LANES, LANES), F32)],
        compiler_params=_cparams(("parallel", "parallel", "arbitrary")),
        name="rwkv_rec",
    )(*arrs, lng, lnb, rk, s0)


def _trunk(x, pos0, diff_k_past, diff_v_past, pool_hist, rw_shift, rw_state, mem_k, mem_v, W, cfg):
    b, t, d = x.shape
    n = b * t
    depth = W["norm_mix_g"].shape[0]
    tm, tm_rw, tm_xa = cfg["tm"], cfg["tm_rw"], cfg["tm_xa"]
    x2 = x.reshape(n, d)
    new_k, new_v, new_pool, new_shift, new_s = [], [], [], [], []
    v_first = None
    half = d // 2
    heads = half // LANES
    for l in range(depth):
        if l % 2 == 0:
            e = l // 2
            prompt = diff_k_past is None
            outs = _pre_even(x2, W["norm_mix_g"], W["ev_w_in"], l, e, tm, t, transposed=prompt)
            u, k_rows, v_rows = outs[:3]
            new_k.append(k_rows.reshape(b, t, heads, 2 * DIFF_DH))
            new_v.append(v_rows.reshape(b, t, heads, DIFF_VD))
            u3 = u.reshape(b, t, half)
            new_pool.append(u3[:, t - POOL_HIST:])
            hist16 = jnp.pad(pool_hist[e], ((0, 0), (POOL_HALO - POOL_HIST, 0), (0, 0)))
            m1 = _pool(u3, hist16, W["ev_pool_w"], W["ev_pool_scale"], e, cfg["tp"], pos0).reshape(n, half)
            lam_init = 0.8 - 0.6 * math.exp(-0.3 * l)
            lam_vecs = [W[nm] for nm in ("ev_lam_q1", "ev_lam_k1", "ev_lam_q2", "ev_lam_k2")]
            if prompt:
                kb, qt, vt = outs[3:]
                m2 = _diff_prompt(qt, kb.reshape(b, t, half), vt, lam_vecs, W["ev_subln_g"], e, lam_init,
                                  cfg["tq"], cfg["hps"])
            else:
                m2 = _diff_sample(outs[3].reshape(b, t, half), diff_k_past, diff_v_past,
                                  k_rows.reshape(b, t * heads, LANES), v_rows.reshape(b, t * heads, LANES),
                                  lam_vecs, W["ev_subln_g"], e, lam_init, cfg["tk"])
            m1, m2 = (m1, 0), (m2.reshape(n, half), 0)
            w_mix, lmix = W["ev_w_out"], e
        else:
            o = l // 2
            rows_first = jnp.broadcast_to(rw_shift[o][:, None, :], (b, min(t, tm_rw), d)).reshape(-1, d)
            outs = _pre_rwkv(x2, rows_first, W["norm_mix_g"], W, l, o, None if o == 0 else v_first, tm_rw, t,
                             cfg["hl_rows"])
            r, k, v, kr, ag, wl, gate, hl = outs
            if o == 0:
                v_first = v
            if cfg["hl_rows"] == tm_rw:
                new_shift.append(hl.reshape(b, t, d)[:, -1])
            else:
                per_seq = t // tm_rw
                new_shift.append(hl.reshape(b, per_seq, cfg["hl_rows"], d)[:, -1, -1])
            arrs = [a.reshape(b, t, d) for a in (r, k, v, kr, ag, wl, gate)]
            y, s_fin = _rwkv_rec(arrs, W["rw_lnx_g"], W["rw_lnx_b"], W["rw_r_k"], rw_state, o, cfg["L"],
                                 cfg["pairs"], cfg["nb"])
            new_s.append(s_fin)
            y2 = y.reshape(n, d)
            m1, m2 = (y2, 0), (y2, 1)
            w_mix, lmix = W["rw_wo"], o
        ffn = (W["norm_ffn_g"], W["ffn_wg"], W["ffn_wu"], W["ffn_wd"], W["final_norm_g"], l == depth - 1)
        if cfg["fuse_ffn"]:
            x2 = _mix_xa(x2, m1, m2, w_mix, lmix, W["norm_xa_g"], W["xa_wq"], mem_k, mem_v, W["xa_wo"], l, tm_xa,
                         t, ffn=ffn)
        else:
            x2 = _mix_xa(x2, m1, m2, w_mix, lmix, W["norm_xa_g"], W["xa_wq"], mem_k, mem_v, W["xa_wo"], l, tm_xa,
                         t)
            x2 = _ffn(x2, *ffn[:5], l, tm, final=ffn[5])
    return (x2.reshape(b, t, d), jnp.stack(new_k), jnp.stack(new_v), jnp.stack(new_pool),
            jnp.stack(new_shift), jnp.stack(new_s))


def kernel(x_prompt, x_sample, cache_diff_k, cache_diff_v, state_pool, state_rw_shift, state_rw_wkv, cache_mem_k, cache_mem_v, mem_prompt, norm_mix_g, norm_xa_g, norm_ffn_g, final_norm_g, ev_w_in, ev_pool_w, ev_pool_scale, ev_lam_q1, ev_lam_k1, ev_lam_q2, ev_lam_k2, ev_subln_g, ev_w_out, rw_mu, rw_wr, rw_wk, rw_wv, rw_wo, rw_w0, rw_w1, rw_w2, rw_a0, rw_a1, rw_a2, rw_v0, rw_v1, rw_v2, rw_g1, rw_g2, rw_k_k, rw_k_a, rw_r_k, rw_lnx_g, rw_lnx_b, xa_wq, xa_wk, xa_wv, xa_wo, ffn_wg, ffn_wu, ffn_wd):
    bf = lambda a: a.astype(BF16)
    vec = lambda a: a.reshape(a.shape[0], 1, -1)
    W = dict(
        norm_mix_g=vec(norm_mix_g), norm_xa_g=vec(norm_xa_g), norm_ffn_g=vec(norm_ffn_g),
        final_norm_g=final_norm_g.reshape(1, -1),
        ev_w_in=bf(ev_w_in), ev_pool_w=bf(ev_pool_w), ev_pool_scale=vec(ev_pool_scale),
        ev_lam_q1=vec(ev_lam_q1), ev_lam_k1=vec(ev_lam_k1), ev_lam_q2=vec(ev_lam_q2), ev_lam_k2=vec(ev_lam_k2),
        ev_subln_g=vec(ev_subln_g), ev_w_out=bf(ev_w_out),
        rw_mu=rw_mu, rw_wr=bf(rw_wr), rw_wk=bf(rw_wk), rw_wv=bf(rw_wv), rw_wo=bf(rw_wo),
        rw_w0=vec(rw_w0), rw_w1=bf(rw_w1), rw_w2=bf(rw_w2), rw_a0=vec(rw_a0), rw_a1=bf(rw_a1), rw_a2=bf(rw_a2),
        rw_v0=vec(rw_v0), rw_v1=bf(rw_v1), rw_v2=bf(rw_v2), rw_g1=bf(rw_g1), rw_g2=bf(rw_g2),
        rw_k_k=vec(rw_k_k), rw_k_a=vec(rw_k_a), rw_r_k=vec(rw_r_k),
        rw_lnx_g=vec(rw_lnx_g), rw_lnx_b=vec(rw_lnx_b),
        xa_wq=bf(xa_wq), xa_wo=bf(xa_wo), ffn_wg=bf(ffn_wg), ffn_wu=bf(ffn_wu), ffn_wd=bf(ffn_wd),
    )
    depth, d = norm_mix_g.shape
    bp, tp_len = x_prompt.shape[:2]
    bs, ts = x_sample.shape[:2]
    n_even = ev_w_in.shape[0]
    n_odd = rw_wr.shape[0]
    nmem = mem_prompt.shape[1]

    kv_f32, kv_bf = _mem_kv(mem_prompt.reshape(bp * nmem, d), bf(jnp.concatenate([xa_wk, xa_wv], axis=0)))
    p_mem_k = kv_f32[:depth].reshape(depth, bp, nmem, XA_HEADS, d // XA_HEADS)
    p_mem_v = kv_f32[depth:].reshape(depth, bp, nmem, XA_HEADS, d // XA_HEADS)
    pmk = kv_bf[:depth].reshape(depth, bp, nmem, d)
    pmv = kv_bf[depth:].reshape(depth, bp, nmem, d)

    cfg_p = dict(tm=512, tm_rw=256, tm_xa=512, tp=512, tq=512, hps=4, tk=None, L=64, pairs=8, nb=2, hl_rows=8,
                 fuse_ffn=True)
    zero_pool = jnp.zeros((n_even, bp, POOL_HIST, d // 2), F32)
    zero_shift = jnp.zeros((n_odd, bp, d), F32)
    zero_wkv = jnp.zeros((n_odd, bp, d // RW_N, RW_N, RW_N), F32)
    outs_p = _trunk(x_prompt, 0, None, None, zero_pool, zero_shift, zero_wkv, pmk, pmv, W, cfg_p)

    past = cache_diff_k.shape[2]
    cfg_s = dict(tm=bs * ts, tm_rw=bs * ts, tm_xa=4 * ts, tp=ts, tq=None, hps=None, tk=2048, L=ts, pairs=8, nb=2,
                 hl_rows=bs * ts, fuse_ffn=False)
    outs_s = _trunk(x_sample, past, cache_diff_k, cache_diff_v, state_pool, state_rw_shift, state_rw_wkv,
                    cache_mem_k, cache_mem_v, W, cfg_s)

    y_p, pk, pv, pp, psh, pS = outs_p
    y_s, sk, sv, sp, ssh, sS = outs_s
    return (y_p, y_s, pk, pv, pp, psh, pS, p_mem_k, p_mem_v, sk, sv, sp, ssh, sS)
```

```python
import functools
import math

import jax
import jax.numpy as jnp
import numpy as np
from jax import lax
from jax.experimental import pallas as pl
from jax.experimental.pallas import tpu as pltpu

F32 = jnp.float32
BF16 = jnp.bfloat16

CHUNK = 64
POOL_GROUPS = 4
POOL_WINDOWS = (2, 4, 8, 16)
POOL_HIST = max(POOL_WINDOWS) - 1
POOL_HALO = 16
DIFF_DH = 64
DIFF_VD = 2 * DIFF_DH
RW_N = 64
RW_LN_EPS = 64e-5
XA_HEADS = 4
NORM_EPS = 1e-6
NEG_INF = -1e30
LOG2E = math.log2(math.e)
LANES = 128

VMEM_LIMIT = 56 * 1024 * 1024


def _cparams(sem):
    return pltpu.CompilerParams(dimension_semantics=sem, vmem_limit_bytes=VMEM_LIMIT)


def _layer_spec(arr, layer, block=None, index=None):
    block = tuple(arr.shape[1:]) if block is None else tuple(block)
    index = (0,) * len(block) if index is None else tuple(index)
    return pl.BlockSpec((None,) + block, lambda *_: (layer,) + index, pipeline_mode=pl.Buffered(1))


def _bdot(a, b):
    return jnp.dot(a.astype(BF16), b.astype(BF16), preferred_element_type=F32)


def _bdot_nt(a, b):
    return lax.dot_general(a.astype(BF16), b.astype(BF16), (((1,), (1,)), ((), ())),
                           preferred_element_type=F32)


def _bdot_tn(a, b):
    return lax.dot_general(a.astype(BF16), b.astype(BF16), (((0,), (0,)), ((), ())),
                           preferred_element_type=F32)


def _rms(x, g):
    return x * lax.rsqrt(jnp.mean(x * x, axis=-1, keepdims=True) + NORM_EPS) * g


def _sigmoid(x):
    return 1.0 / (1.0 + jnp.exp(-x))


def _log2(c):
    assert c > 0 and c & (c - 1) == 0, c
    return c.bit_length() - 1


def _div_pow2(x, c):
    return lax.shift_right_arithmetic(x, _log2(c))


def _mod_pow2(x, c):
    assert c & (c - 1) == 0, c
    return x & (c - 1)


def _memkv_kernel(x_ref, w_ref, o_ref, ob_ref):
    y = _bdot(x_ref[...], w_ref[0])
    o_ref[0] = y
    ob_ref[0] = y.astype(BF16)


def _mem_kv(mem2d, w_stack):
    n, d = mem2d.shape
    nw = w_stack.shape[0]
    return pl.pallas_call(
        _memkv_kernel,
        grid=(nw,),
        in_specs=[pl.BlockSpec((n, d), lambda i: (0, 0)),
                  pl.BlockSpec((1, d, d), lambda i: (i, 0, 0))],
        out_specs=[pl.BlockSpec((1, n, d), lambda i: (i, 0, 0)),
                   pl.BlockSpec((1, n, d), lambda i: (i, 0, 0))],
        out_shape=[jax.ShapeDtypeStruct((nw, n, d), F32),
                   jax.ShapeDtypeStruct((nw, n, d), BF16)],
        compiler_params=_cparams(("parallel",)),
        name="mem_kv",
    )(mem2d, w_stack)


def _pre_even_kernel(x_ref, g_ref, w_ref, u_ref, k_ref, v_ref, *rest, pw, dw, transposed):
    h = _rms(x_ref[...], g_ref[...]).astype(BF16)
    tm = h.shape[0]
    heads = dw // LANES
    proj = lambda lo: jnp.dot(h, w_ref[:, lo:lo + dw], preferred_element_type=F32)
    hsl = [slice(hd * LANES, (hd + 1) * LANES) for hd in range(heads)]
    v = proj(pw + 2 * dw)
    if transposed:
        kb_ref, qt_ref, vt_ref = rest
        for hd in range(heads):
            vt_ref[0, hd] = v[:, hsl[hd]].T.astype(BF16)
    for hd in range(heads):
        v_ref[pl.ds(hd, tm, stride=heads), :] = v[:, hsl[hd]]
    q = proj(pw) * (DIFF_DH ** -0.5)
    if transposed:
        for hd in range(heads):
            qt_ref[0, hd] = (q[:, hsl[hd]] * LOG2E).T.astype(BF16)
    else:
        (q_ref,) = rest
        q_ref[...] = q.astype(BF16)
    k = proj(pw + dw)
    for hd in range(heads):
        k_ref[pl.ds(hd, tm, stride=heads), :] = k[:, hsl[hd]]
    if transposed:
        kb_ref[...] = k.astype(BF16)
    u_ref[...] = proj(0)


def _pre_even(x2d, g, w_in, layer, e, tm, seq_len, transposed):
    n, d = x2d.shape
    pw = d // 2
    dw = d // 2
    heads = dw // LANES
    row = lambda c: pl.BlockSpec((tm, c), lambda i: (i, 0))
    kv_rows = pl.BlockSpec((tm * heads, LANES), lambda i: (i, 0))
    out_specs = [row(pw), kv_rows, kv_rows]
    out_shape = [jax.ShapeDtypeStruct((n, pw), F32), jax.ShapeDtypeStruct((n * heads, LANES), F32),
                 jax.ShapeDtypeStruct((n * heads, LANES), F32)]
    if transposed:
        tps = seq_len // tm
        tspec = pl.BlockSpec((1, heads, LANES, tm), lambda i: (i // tps, 0, 0, i % tps))
        tshape = jax.ShapeDtypeStruct((n // seq_len, heads, LANES, seq_len), BF16)
        out_specs += [row(dw), tspec, tspec]
        out_shape += [jax.ShapeDtypeStruct((n, dw), BF16), tshape, tshape]
    else:
        out_specs += [row(dw)]
        out_shape += [jax.ShapeDtypeStruct((n, dw), BF16)]
    return pl.pallas_call(
        functools.partial(_pre_even_kernel, pw=pw, dw=dw, transposed=transposed),
        grid=(n // tm,),
        in_specs=[row(d), _layer_spec(g, layer), _layer_spec(w_in, e)],
        out_specs=out_specs,
        out_shape=out_shape,
        compiler_params=_cparams(("parallel",)),
        name="pre_even",
    )(x2d, g, w_in)


def _pool_kernel(u_ref, uprev_ref, hist_ref, w_ref, sc_ref, o_ref, *, tp, pos0, gw, nseq):
    t = pl.program_id(1)
    pos = pos0 + t * tp + lax.broadcasted_iota(jnp.int32, (tp, 1), 0)
    pooled = [[] for _ in POOL_WINDOWS]
    for b in range(nseq):
        u = u_ref[b]
        prev = jnp.where(t == 0, hist_ref[b], uprev_ref[b])
        s = jnp.concatenate([prev, u], axis=0)
        for g, (sh, w) in enumerate(zip((1, 2, 4, 8), POOL_WINDOWS)):
            s = s + pltpu.roll(s, sh, 0)
            sl = slice(g * gw, (g + 1) * gw)
            cnt = jnp.minimum(pos + 1, w).astype(F32)
            pooled[g].append(s[POOL_HALO:, sl] / cnt - u[:, sl])
    for g in range(len(POOL_WINDOWS)):
        sl = slice(g * gw, (g + 1) * gw)
        y = _bdot(jnp.concatenate(pooled[g], axis=0), w_ref[g]) * sc_ref[:, sl]
        for b in range(nseq):
            o_ref[b, :, sl] = y[b * tp:(b + 1) * tp].astype(BF16)


def _pool(u3, hist16, pool_w, scale, e, tp, pos0, nseq):
    b, t, pw = u3.shape
    gw = pw // POOL_GROUPS
    hb = tp // POOL_HALO
    return pl.pallas_call(
        functools.partial(_pool_kernel, tp=tp, pos0=pos0, gw=gw, nseq=nseq),
        grid=(b // nseq, t // tp),
        in_specs=[pl.BlockSpec((nseq, tp, pw), lambda i, j: (i, j, 0)),
                  pl.BlockSpec((nseq, POOL_HALO, pw), lambda i, j: (i, jnp.maximum(j * hb - 1, 0), 0)),
                  pl.BlockSpec((nseq, POOL_HALO, pw), lambda i, j: (i, 0, 0)),
                  _layer_spec(pool_w, e), _layer_spec(scale, e)],
        out_specs=pl.BlockSpec((nseq, tp, pw), lambda i, j: (i, j, 0)),
        out_shape=jax.ShapeDtypeStruct((b, t, pw), BF16),
        compiler_params=_cparams(("parallel", "parallel")),
        name="pool_mixer",
    )(u3, u3, hist16, pool_w, scale)


def _lam(lq1, lk1, lq2, lk2, lam_init):
    return (jnp.exp(jnp.sum(lq1[...] * lk1[...], axis=-1, keepdims=True))
            - jnp.exp(jnp.sum(lq2[...] * lk2[...], axis=-1, keepdims=True)) + lam_init)


def _stack_maps(q):
    lane = lax.broadcasted_iota(jnp.int32, q.shape, 1)
    zero = jnp.zeros_like(q)
    return jnp.concatenate([jnp.where(lane < DIFF_DH, q, zero), jnp.where(lane >= DIFF_DH, q, zero)], axis=0)


def _softmax_step(s, v, m_prev, l_prev, acc_prev):
    m_new = jnp.maximum(m_prev, jnp.max(s, axis=-1, keepdims=True))
    alpha = jnp.exp(m_prev - m_new)
    p = jnp.exp(s - m_new)
    l_new = alpha * l_prev + jnp.sum(p, axis=-1, keepdims=True)
    acc_new = alpha * acc_prev + _bdot(p, v)
    return m_new, l_new, acc_new


def _diff_finish(l, acc, lam, subg, rows, out_scale):
    o = acc[:rows] / l[:rows] - lam * (acc[rows:] / l[rows:])
    o = o * lax.rsqrt(jnp.mean(o * o, axis=-1, keepdims=True) + NORM_EPS) * subg
    return o * out_scale


SCORES_AHEAD = 2


def _diffp_kernel(qi_ref, kj_ref, qt_ref, k_ref, vt_ref, bias_ref, lq1, lk1, lq2, lk2, subg_ref, o_ref,
                  q2_ref, m_ref, l_ref, acc_ref, *, tq, hps, lam_init):
    step_id = pl.program_id(2)
    i = qi_ref[step_id]
    j = kj_ref[step_id]

    @pl.when(j == 0)
    def _():
        for hh in range(hps):
            qt = qt_ref[0, hh]
            row = lax.broadcasted_iota(jnp.int32, qt.shape, 0)
            zero = jnp.zeros_like(qt)
            q2_ref[hh, :, :tq] = jnp.where(row < DIFF_DH, qt, zero)
            q2_ref[hh, :, tq:] = jnp.where(row >= DIFF_DH, qt, zero)
        m_ref[...] = jnp.full(m_ref.shape, NEG_INF, F32)
        l_ref[...] = jnp.zeros(l_ref.shape, F32)
        acc_ref[...] = jnp.zeros(acc_ref.shape, F32)

    def step(masked):
        def scores(hh):
            s = jnp.dot(k_ref[0, :, hh * LANES:(hh + 1) * LANES], q2_ref[hh],
                        preferred_element_type=F32)
            return s + bias_ref[...] if masked else s

        def softmax(hh, s):
            m_prev = m_ref[hh]
            m_new = jnp.maximum(m_prev, jnp.max(s, axis=0, keepdims=True))
            alpha = jnp.exp2(m_prev - m_new)
            p = jnp.exp2(s - m_new)
            l_ref[hh] = alpha * l_ref[hh] + jnp.sum(p, axis=0, keepdims=True)
            m_ref[hh] = m_new
            return alpha, p.astype(BF16)

        def values(hh, alpha, p):
            acc_ref[hh] = alpha * acc_ref[hh] + jnp.dot(vt_ref[0, hh], p, preferred_element_type=F32)

        ready = [scores(hh) for hh in range(min(SCORES_AHEAD, hps))]
        pending = None
        for hh in range(hps):
            s_cur = ready.pop(0)
            if hh + SCORES_AHEAD < hps:
                ready.append(scores(hh + SCORES_AHEAD))
            cur = softmax(hh, s_cur)
            if pending is not None:
                values(hh - 1, *pending)
            pending = cur
        values(hps - 1, *pending)

    @pl.when(j < i)
    def _():
        step(False)

    @pl.when(j == i)
    def _():
        step(True)
        lam = _lam(lq1, lk1, lq2, lk2, lam_init)
        for hh in range(hps):
            acc = acc_ref[hh]
            l = l_ref[hh]
            ot = acc[:, :tq] / l[:, :tq] - lam * (acc[:, tq:] / l[:, tq:])
            ot = ot * lax.rsqrt(jnp.mean(ot * ot, axis=0, keepdims=True) + NORM_EPS)
            o_ref[0, :, hh * LANES:(hh + 1) * LANES] = (ot.T * subg_ref[...] * (1.0 - lam_init)).astype(BF16)


def _diff_prompt(qt, kb3, vt, lam_vecs, subg, e, lam_init, tq, hps):
    b, t, w = kb3.shape
    heads = w // LANES
    nq = t // tq
    pairs = [(i, j) for i in range(nq) for j in range(i + 1)]
    qi = jnp.asarray(np.array([p[0] for p in pairs], np.int32))
    kj = jnp.asarray(np.array([p[1] for p in pairs], np.int32))
    key_chunk = np.arange(tq)[:, None] // CHUNK
    query_chunk = (np.arange(2 * tq)[None, :] % tq) // CHUNK
    bias = jnp.asarray(np.where(key_chunk <= query_chunk, 0.0, NEG_INF).astype(np.float32))
    qspec = pl.BlockSpec((1, hps, LANES, tq), lambda bb, h, s, qi_r, kj_r: (bb, h, 0, qi_r[s]))
    kspec = pl.BlockSpec((1, tq, hps * LANES), lambda bb, h, s, qi_r, kj_r: (bb, kj_r[s], h))
    vspec = pl.BlockSpec((1, hps, LANES, tq), lambda bb, h, s, qi_r, kj_r: (bb, h, 0, kj_r[s]))
    vec = lambda a: pl.BlockSpec((None,) + tuple(a.shape[1:]), lambda *_: (e, 0, 0))
    grid_spec = pltpu.PrefetchScalarGridSpec(
        num_scalar_prefetch=2,
        grid=(b, heads // hps, len(pairs)),
        in_specs=[qspec, kspec, vspec,
                  pl.BlockSpec(bias.shape, lambda *_: (0, 0), pipeline_mode=pl.Buffered(1))]
        + [vec(a) for a in lam_vecs] + [vec(subg)],
        out_specs=pl.BlockSpec((1, tq, hps * LANES), lambda bb, h, s, qi_r, kj_r: (bb, qi_r[s], h)),
        scratch_shapes=[pltpu.VMEM((hps, LANES, 2 * tq), BF16), pltpu.VMEM((hps, 1, 2 * tq), F32),
                        pltpu.VMEM((hps, 1, 2 * tq), F32), pltpu.VMEM((hps, LANES, 2 * tq), F32)],
    )
    return pl.pallas_call(
        functools.partial(_diffp_kernel, tq=tq, hps=hps, lam_init=lam_init),
        grid_spec=grid_spec,
        out_shape=jax.ShapeDtypeStruct((b, t, w), BF16),
        compiler_params=_cparams(("parallel", "parallel", "arbitrary")),
        name="diff_attn_prompt",
    )(qi, kj, qt, kb3, vt, bias, *lam_vecs, subg)


def _diffs_kernel(q_ref, ck_ref, cv_ref, kn_ref, vn_ref, lq1, lk1, lq2, lk2, subg_ref, o_ref,
                  m_ref, l_ref, acc_ref, *, ts, heads, lam_init):
    j = pl.program_id(1)
    nj = pl.num_programs(1)
    nq = heads * 2 * ts
    q2 = jnp.concatenate([_stack_maps(q_ref[0, :, h * LANES:(h + 1) * LANES]) for h in range(heads)], axis=0)

    def visible(ncols, valid_cols):
        qh = _div_pow2(lax.broadcasted_iota(jnp.int32, (nq, ncols), 0), 2 * ts)
        col = lax.broadcasted_iota(jnp.int32, (nq, ncols), 1)
        ok = _mod_pow2(col, heads) == qh
        return ok if valid_cols is None else ok & (col < valid_cols)

    @pl.when(j == 0)
    def _():
        nnew = kn_ref.shape[1]
        pad = jnp.zeros((LANES - nnew, LANES), F32)
        kn = jnp.concatenate([kn_ref[0], pad], axis=0)
        vn = jnp.concatenate([vn_ref[0], pad], axis=0)
        s = jnp.where(visible(LANES, nnew), _bdot_nt(q2, kn), NEG_INF)
        m, l, acc = _softmax_step(s, vn, jnp.full((nq, 1), NEG_INF, F32), jnp.zeros((nq, 1), F32),
                                  jnp.zeros((nq, LANES), F32))
        m_ref[...] = m
        l_ref[...] = l
        acc_ref[...] = acc

    s = _bdot_nt(q2, ck_ref[...])
    s = jnp.where(visible(s.shape[1], None), s, NEG_INF)
    m, l, acc = _softmax_step(s, cv_ref[...], m_ref[...], l_ref[...], acc_ref[...])
    m_ref[...] = m
    l_ref[...] = l
    acc_ref[...] = acc

    @pl.when(j == nj - 1)
    def _():
        lam = _lam(lq1, lk1, lq2, lk2, lam_init)
        l_all = l_ref[...]
        acc_all = acc_ref[...]
        for h in range(heads):
            rows = slice(h * 2 * ts, (h + 1) * 2 * ts)
            o_ref[0, :, h * LANES:(h + 1) * LANES] = _diff_finish(
                l_all[rows], acc_all[rows], lam, subg_ref[...], ts, 1.0 - lam_init).astype(BF16)


def _diff_sample(q3, ck5, cv5, kn_rows, vn_rows, lam_vecs, subg, e, lam_init, tk):
    b, ts, w = q3.shape
    nl, _, past, heads, _ = ck5.shape
    assert heads * LANES == w and heads * ts <= LANES
    assert past % CHUNK == 0 and ts <= CHUNK and past % tk == 0
    rows = lambda a: a.reshape(a.shape[:-3] + (a.shape[-3] * heads, LANES))
    qspec = pl.BlockSpec((1, ts, w), lambda bb, j: (bb, 0, 0))
    new = pl.BlockSpec((1, ts * heads, LANES), lambda bb, j: (bb, 0, 0))
    cache = pl.BlockSpec((None, None, tk * heads, LANES), lambda bb, j: (e, bb, j, 0))
    vec = lambda a: pl.BlockSpec((None,) + tuple(a.shape[1:]), lambda *_: (e, 0, 0))
    nq = heads * 2 * ts
    return pl.pallas_call(
        functools.partial(_diffs_kernel, ts=ts, heads=heads, lam_init=lam_init),
        grid=(b, past // tk),
        in_specs=[qspec, cache, cache, new, new] + [vec(a) for a in lam_vecs] + [vec(subg)],
        out_specs=qspec,
        out_shape=jax.ShapeDtypeStruct((b, ts, w), BF16),
        scratch_shapes=[pltpu.VMEM((nq, 1), F32), pltpu.VMEM((nq, 1), F32), pltpu.VMEM((nq, LANES), F32)],
        compiler_params=_cparams(("parallel", "arbitrary")),
        name="diff_attn_sample",
    )(q3, rows(ck5), rows(cv5), kn_rows, vn_rows, *lam_vecs, subg)


def _mixxa_compute(x_ref, m1_ref, m2_ref, w1_ref, w2_ref, g_ref, wq_ref, mk_ref, mv_ref, wo_ref, *, nseq, heads):
    x1 = x_ref[...] + _bdot(m1_ref[...], w1_ref[...]) + _bdot(m2_ref[...], w2_ref[...])
    tm, d = x1.shape
    dh = d // heads
    h = _rms(x1, g_ref[...])
    q = (_bdot(h, wq_ref[...]) * (dh ** -0.5)).astype(BF16)
    rq = tm // nseq
    if len(mk_ref.shape) == 4:
        nrows = mk_ref.shape[1] * heads
        qh = _div_pow2(lax.broadcasted_iota(jnp.int32, (heads * rq, nrows), 0), rq)
        own = _mod_pow2(lax.broadcasted_iota(jnp.int32, (heads * rq, nrows), 1), heads) == qh
        outs = []
        for b in range(nseq):
            qb = q[b * rq:(b + 1) * rq]
            qcat = jnp.concatenate([qb[:, hd * dh:(hd + 1) * dh] for hd in range(heads)], axis=0)
            s = jnp.where(own, _bdot_nt(qcat, mk_ref[b].reshape(nrows, dh)), NEG_INF)
            p = jnp.exp(s - jnp.max(s, axis=-1, keepdims=True))
            p = p / jnp.sum(p, axis=-1, keepdims=True)
            ob = _bdot(p, mv_ref[b].reshape(nrows, dh))
            outs.append(jnp.concatenate([ob[hd * rq:(hd + 1) * rq] for hd in range(heads)], axis=1))
        o = jnp.concatenate(outs, axis=0)
    else:
        q3 = q.reshape(nseq, rq, d)
        outs = []
        for hd in range(heads):
            sl = slice(hd * dh, (hd + 1) * dh)
            s = jnp.einsum("bqd,bkd->bqk", q3[:, :, sl], mk_ref[:, :, sl], preferred_element_type=F32)
            p = jnp.exp(s - jnp.max(s, axis=-1, keepdims=True))
            p = p / jnp.sum(p, axis=-1, keepdims=True)
            outs.append(jnp.einsum("bqk,bkd->bqd", p.astype(BF16), mv_ref[:, :, sl], preferred_element_type=F32))
        o = jnp.concatenate(outs, axis=-1).reshape(tm, d)
    return x1 + _bdot(o, wo_ref[...])


def _mixxa_kernel(*refs, nseq, heads):
    refs[-1][...] = _mixxa_compute(*refs[:-1], nseq=nseq, heads=heads)


def _mixxa_ffn_kernel(*refs, nseq, heads, nchunk, final):
    x2 = _mixxa_compute(*refs[:10], nseq=nseq, heads=heads)
    refs[-1][...] = _ffn_compute(x2, *refs[10:-1], nchunk=nchunk, final=final)


FFN_CHUNKS = 11


def _mix_xa(x2d, m1, m2, w_mix, lmix, g, wq, mk, mv, wo, layer, tm, rows_per_seq, ffn=None):
    n, d = x2d.shape
    half = d // 2
    nseq = max(tm // rows_per_seq, 1)
    tiles_per_seq = max(rows_per_seq // tm, 1)
    row = lambda c: pl.BlockSpec((tm, c), lambda i: (i, 0))
    col = lambda c: pl.BlockSpec((tm, half), lambda i: (i, c))
    tail = (0,) * (mk.ndim - 2)
    mem = pl.BlockSpec((None, nseq) + mk.shape[2:], lambda i: (layer, i // tiles_per_seq) + tail)
    ins = [x2d, m1[0], m2[0], w_mix, w_mix, g, wq, mk, mv, wo]
    specs = [row(d), col(m1[1]), col(m2[1]),
             _layer_spec(w_mix, lmix, (half, d), (0, 0)), _layer_spec(w_mix, lmix, (half, d), (1, 0)),
             _layer_spec(g, layer), _layer_spec(wq, layer), mem, mem, _layer_spec(wo, layer)]
    if ffn is None:
        body = functools.partial(_mixxa_kernel, nseq=nseq, heads=XA_HEADS)
    else:
        gf, wg, wu, wd, gfin, final = ffn
        ins += [gf, wg, wu, wd, gfin]
        specs += [_layer_spec(a, layer) for a in (gf, wg, wu, wd)] + [pl.BlockSpec((1, d), lambda i: (0, 0))]
        body = functools.partial(_mixxa_ffn_kernel, nseq=nseq, heads=XA_HEADS, nchunk=FFN_CHUNKS, final=final)
    return pl.pallas_call(
        body,
        grid=(n // tm,),
        in_specs=specs,
        out_specs=row(d),
        out_shape=jax.ShapeDtypeStruct((n, d), F32),
        compiler_params=_cparams(("parallel",)),
        name="mix_xa" if ffn is None else "mix_xa_ffn",
    )(*ins)


def _ffn_compute(x, g_ref, wg_ref, wu_ref, wd_ref, gf_ref, *, nchunk, final):
    h = _rms(x, g_ref[...]).astype(BF16)
    ff = wg_ref.shape[1]
    fc = ff // nchunk
    acc = x
    for c in range(nchunk):
        sl = slice(c * fc, (c + 1) * fc)
        a = jnp.dot(h, wg_ref[:, sl], preferred_element_type=F32)
        b = jnp.dot(h, wu_ref[:, sl], preferred_element_type=F32)
        acc = acc + _bdot(a * _sigmoid(a) * b, wd_ref[sl, :])
    if final:
        acc = _rms(acc, gf_ref[...])
    return acc


def _ffn_kernel(x_ref, *refs, nchunk, final):
    refs[-1][...] = _ffn_compute(x_ref[...], *refs[:-1], nchunk=nchunk, final=final)


def _ffn(x2d, g, wg, wu, wd, gf, layer, tm, final):
    n, d = x2d.shape
    row = pl.BlockSpec((tm, d), lambda i: (i, 0))
    return pl.pallas_call(
        functools.partial(_ffn_kernel, nchunk=FFN_CHUNKS, final=final),
        grid=(n // tm,),
        in_specs=[row, _layer_spec(g, layer), _layer_spec(wg, layer), _layer_spec(wu, layer),
                  _layer_spec(wd, layer), pl.BlockSpec((1, d), lambda i: (0, 0))],
        out_specs=row,
        out_shape=jax.ShapeDtypeStruct((n, d), F32),
        compiler_params=_cparams(("parallel",)),
        name="ffn",
    )(x2d, g, wg, wu, wd, gf)


_RW_PARAMS = ("mu", "wr", "wk", "wv", "w0", "w1", "w2", "a0", "a1", "a2", "g1", "g2", "k_k", "k_a")


def _pre_rwkv_kernel(*refs, tm, rows_per_seq, hl_rows, has_vres):
    (x_ref, xp_ref, first_ref, g_ref, mu_ref, wr_ref, wk_ref, wv_ref, w0_ref, w1_ref, w2_ref,
     a0_ref, a1_ref, a2_ref, g1_ref, g2_ref, kk_ref, ka_ref) = refs[:18]
    pos = 18
    if has_vres:
        vf_ref, v0_ref, v1_ref, v2_ref = refs[pos:pos + 4]
        pos += 4
    r_ref, k_ref, v_ref, kr_ref, ag_ref, w_ref, gate_ref, hl_ref = refs[pos:]

    i = pl.program_id(0)
    g = g_ref[...]
    h = _rms(x_ref[...], g)
    hp_last = _rms(xp_ref[...], g)[xp_ref.shape[0] - 1:, :]
    rowl = lax.broadcasted_iota(jnp.int32, (tm, 1), 0)
    shifted = jnp.where(rowl == 0, hp_last, pltpu.roll(h, 1, 0))
    is_start = _mod_pow2(i * tm + rowl, rows_per_seq) == 0
    h_prev = jnp.where(is_start, first_ref[...], shifted)
    xx = h_prev - h
    mix = lambda n: (h + xx * mu_ref[n:n + 1, :]).astype(BF16)
    t_w = _bdot(mix(1), w1_ref[...])
    t_a = _bdot(mix(4), a1_ref[...])
    t_g = _bdot(mix(5), g1_ref[...])
    xv = mix(3)
    if has_vres:
        t_v = _bdot(xv, v1_ref[...])
    r_ref[...] = _bdot(mix(0), wr_ref[...]).astype(r_ref.dtype)
    z = w0_ref[...] + _bdot(jnp.tanh(t_w), w2_ref[...])
    w_ref[...] = -(jnp.maximum(-z, 0.0) + jnp.log1p(jnp.exp(-jnp.abs(z)))) - 0.5
    k = _bdot(mix(2), wk_ref[...])
    a = _sigmoid(a0_ref[...] + _bdot(t_a, a2_ref[...]))
    ag_ref[...] = a.astype(ag_ref.dtype)
    v = _bdot(xv, wv_ref[...])
    if has_vres:
        v = v + (vf_ref[...].astype(F32) - v) * _sigmoid(v0_ref[...] + _bdot(t_v, v2_ref[...]))
    v_ref[...] = v.astype(v_ref.dtype)
    gate_ref[...] = _bdot(_sigmoid(t_g), g2_ref[...]).astype(gate_ref.dtype)
    kr_ref[...] = (k * kk_ref[...]).astype(kr_ref.dtype)
    k_ref[...] = (k * (1.0 + (a - 1.0) * ka_ref[...])).astype(k_ref.dtype)
    hl_ref[...] = h[tm - hl_rows:, :]


def _pre_rwkv(x2d, first, g, W, layer, o, vf, tm, rows_per_seq, hl_rows):
    n, d = x2d.shape
    has_vres = vf is not None
    xp_rows = 8
    row = pl.BlockSpec((tm, d), lambda i: (i, 0))
    ins = [x2d, x2d, first, g] + [W["rw_" + nm] for nm in _RW_PARAMS]
    specs = [row,
             pl.BlockSpec((xp_rows, d), lambda i: (jnp.maximum(i * (tm // xp_rows) - 1, 0), 0)),
             pl.BlockSpec((tm, d), lambda i: (i // max(rows_per_seq // tm, 1), 0)),
             _layer_spec(g, layer)] + [_layer_spec(W["rw_" + nm], o) for nm in _RW_PARAMS]
    if has_vres:
        vres = [W["rw_v0"], W["rw_v1"], W["rw_v2"]]
        ins += [vf] + vres
        specs += [row] + [_layer_spec(a, o - 1) for a in vres]
    nt = n // tm
    dts = [BF16, BF16, BF16, BF16, BF16, F32, BF16]
    return pl.pallas_call(
        functools.partial(_pre_rwkv_kernel, tm=tm, rows_per_seq=rows_per_seq, hl_rows=hl_rows,
                          has_vres=has_vres),
        grid=(nt,),
        in_specs=specs,
        out_specs=[row] * 7 + [pl.BlockSpec((hl_rows, d), lambda i: (i, 0))],
        out_shape=[jax.ShapeDtypeStruct((n, d), dt) for dt in dts]
        + [jax.ShapeDtypeStruct((nt * hl_rows, d), F32)],
        compiler_params=_cparams(("parallel",)),
        name="pre_rwkv",
    )(*ins)


def _split3(x):
    x1 = x.astype(BF16)
    r1 = x - x1.astype(F32)
    x2 = r1.astype(BF16)
    x3 = (r1 - x2.astype(F32)).astype(BF16)
    return x1, x2, x3


def _head_sum(x, lo_mask):
    s_lo = jnp.sum(jnp.where(lo_mask, x, 0.0), axis=-1, keepdims=True)
    s_hi = jnp.sum(jnp.where(lo_mask, 0.0, x), axis=-1, keepdims=True)
    return jnp.where(lo_mask, s_lo, s_hi)


def _rwkv_rec_kernel(r_ref, k_ref, v_ref, kr_ref, ag_ref, w_ref, gate_ref, lng_ref, lnb_ref, rk_ref, s0_ref,
                     y_ref, sfin_ref, s_scr, *, L, pairs, nb):
    c = pl.program_id(2)
    nc = pl.num_programs(2)
    P = range(nb * pairs)
    bi = [p // pairs for p in P]
    hp = [p % pairs for p in P]

    @pl.when(c == 0)
    def _():
        zero = jnp.zeros((RW_N, RW_N), F32)
        for p in P:
            top = jnp.concatenate([s0_ref[bi[p], 2 * hp[p]], zero], axis=1)
            bot = jnp.concatenate([zero, s0_ref[bi[p], 2 * hp[p] + 1]], axis=1)
            s_scr[p] = jnp.concatenate([top, bot], axis=0)

    L2 = 2 * L
    sub = min(16, L)
    nblk = L // sub
    lane = lax.broadcasted_iota(jnp.int32, (L, LANES), 1)
    lo = lane < RW_N
    row2 = lax.broadcasted_iota(jnp.int32, (L2, L2), 0)
    col2 = lax.broadcasted_iota(jnp.int32, (L2, L2), 1)
    same = _div_pow2(row2, L) == _div_pow2(col2, L)
    strict = same & (col2 < row2)
    incl = same & (col2 <= row2)
    diag_blk = _div_pow2(row2, sub) == _div_pow2(col2, sub)
    eye = (row2 == col2).astype(F32)
    tri = (lax.broadcasted_iota(jnp.int32, (L, L), 1) <= lax.broadcasted_iota(jnp.int32, (L, L), 0)).astype(BF16)
    dot = lambda a, b: jnp.dot(a, b, preferred_element_type=F32)

    def stack(x):
        xb = x.astype(BF16)
        zero = jnp.zeros_like(xb)
        return jnp.concatenate([jnp.where(lo, xb, zero), jnp.where(lo, zero, xb)], axis=0)

    sl = [slice(hp[p] * LANES, (hp[p] + 1) * LANES) for p in P]
    r = [r_ref[bi[p], :, sl[p]].astype(F32) for p in P]
    k = [k_ref[bi[p], :, sl[p]].astype(F32) for p in P]
    v = [v_ref[bi[p], :, sl[p]].astype(F32) for p in P]
    logd = [-jnp.exp(w_ref[bi[p], :, sl[p]]) for p in P]
    kk = []
    for p in P:
        kr = kr_ref[bi[p], :, sl[p]].astype(F32)
        kk.append(kr / jnp.maximum(jnp.sqrt(_head_sum(kr * kr, lo)), 1e-12))
    cum = []
    for p in P:
        l1, l2, l3 = _split3(logd[p])
        cum.append(dot(tri, l1) + dot(tri, l2) + dot(tri, l3))
    ar_s, bk_s, v_s, bkh_s, p_last = [], [], [], [], []
    for p in P:
        e_pos = jnp.exp(cum[p])
        e_neg = jnp.exp(-cum[p])
        pl_ = e_pos[L - 1:, :]
        bt = kk[p] * ag_ref[bi[p], :, sl[p]].astype(F32) * e_neg
        kt = k[p] * e_neg
        ar_s.append(jnp.concatenate([stack(-kk[p] * jnp.exp(cum[p] - logd[p])), stack(r[p] * e_pos)], axis=0))
        bk_s.append(jnp.concatenate([stack(bt), stack(kt)], axis=0))
        bkh_s.append(jnp.concatenate([stack(bt * pl_), stack(kt * pl_)], axis=0))
        v_s.append(stack(v[p]))
        p_last.append(pl_)
    s_bd = [s_scr[p] for p in P]
    ss = [_bdot_nt(ar_s[p], s_bd[p]) for p in P]
    g_all = [_bdot_nt(ar_s[p], bk_s[p]) for p in P]
    a_ab = [jnp.where(strict, g_all[p][:L2, :L2], 0.0) for p in P]
    a_akrk = [jnp.concatenate([jnp.where(strict, g_all[p][:L2, L2:], 0.0),
                               jnp.where(incl, g_all[p][L2:, L2:], 0.0)], axis=0) for p in P]
    a_rb = [jnp.where(incl, g_all[p][L2:, :L2], 0.0) for p in P]
    sv = [_bdot(a_akrk[p], v_s[p]) for p in P]
    wmat = [ss[p][:L2] + sv[p][:L2] for p in P]

    a_d = [jnp.where(diag_blk, a_ab[p], 0.0) for p in P]
    pm = [eye + a_d[p] for p in P]
    pw = a_d
    for _ in range(_log2(sub) - 1):
        pw = [_bdot(pw[p], pw[p]) for p in P]
        pm = [pm[p] + _bdot(pw[p], pm[p]) for p in P]
    if nblk > 1:
        xn = [_bdot(pm[p], jnp.concatenate([wmat[p], a_ab[p] - a_d[p]], axis=1)) for p in P]
        u = [xn[p][:, :LANES] for p in P]
        nm = [xn[p][:, LANES:] for p in P]
        steps = _log2(nblk)
        for it in range(steps):
            u = [u[p] + _bdot(nm[p], u[p]) for p in P]
            if it < steps - 1:
                nm = [_bdot(nm[p], nm[p]) for p in P]
    else:
        u = [_bdot(pm[p], wmat[p]) for p in P]

    y2 = [ss[p][L2:] + sv[p][L2:] + _bdot(a_rb[p], u[p]) for p in P]
    for p in P:
        uv = jnp.concatenate([u[p].astype(BF16), v_s[p]], axis=0)
        s_scr[p] = s_bd[p] * p_last[p] + _bdot_tn(uv, bkh_s[p])
    for p in P:
        y = y2[p][:L] + y2[p][L:]
        mean = _head_sum(y, lo) * (1.0 / RW_N)
        yc = y - mean
        var = _head_sum(yc * yc, lo) * (1.0 / RW_N)
        yn = yc * lax.rsqrt(var + RW_LN_EPS) * lng_ref[:, sl[p]] + lnb_ref[:, sl[p]]
        bonus = _head_sum(r[p] * k[p] * rk_ref[:, sl[p]], lo) * v[p]
        y_ref[bi[p], :, sl[p]] = ((yn + bonus) * gate_ref[bi[p], :, sl[p]].astype(F32)).astype(BF16)

    @pl.when(c == nc - 1)
    def _():
        for p in P:
            s = s_scr[p]
            sfin_ref[bi[p], 2 * hp[p]] = s[:RW_N, :RW_N]
            sfin_ref[bi[p], 2 * hp[p] + 1] = s[RW_N:, RW_N:]


def _rwkv_rec(arrs, lng, lnb, rk, s0, o, L, pairs, nb):
    b, t, d = arrs[0].shape
    heads = d // RW_N
    groups = heads // (2 * pairs)
    w = pairs * LANES
    tok = pl.BlockSpec((nb, L, w), lambda bb, gg, c: (bb, c, gg))
    vec = pl.BlockSpec((None, 1, w), lambda bb, gg, c: (o, 0, gg))
    st_in = pl.BlockSpec((None, nb, 2 * pairs, RW_N, RW_N), lambda bb, gg, c: (o, bb, gg, 0, 0))
    st_out = pl.BlockSpec((nb, 2 * pairs, RW_N, RW_N), lambda bb, gg, c: (bb, gg, 0, 0))
    return pl.pallas_call(
        functools.partial(_rwkv_rec_kernel, L=L, pairs=pairs, nb=nb),
        grid=(b // nb, groups, t // L),
        in_specs=[tok] * 7 + [vec, vec, vec, st_in],
        out_specs=[tok, st_out],
        out_shape=[jax.ShapeDtypeStruct((b, t, d), BF16),
                   jax.ShapeDtypeStruct((b, heads, RW_N, RW_N), F32)],
        scratch_shapes=[pltpu.VMEM((nb * pairs, LANES, LANES), F32)],
        compiler_params=_cparams(("parallel", "parallel", "arbitrary")),
        name="rwkv_rec",
    )(*arrs, lng, lnb, rk, s0)


def _trunk(x, pos0, diff_k_past, diff_v_past, pool_hist, rw_shift, rw_state, mem_k, mem_v, W, cfg):
    b, t, d = x.shape
    n = b * t
    depth = W["norm_mix_g"].shape[0]
    tm, tm_rw, tm_xa = cfg["tm"], cfg["tm_rw"], cfg["tm_xa"]
    x2 = x.reshape(n, d)
    new_k, new_v, new_pool, new_shift, new_s = [], [], [], [], []
    v_first = None
    half = d // 2
    heads = half // LANES
    for l in range(depth):
        if l % 2 == 0:
            e = l // 2
            prompt = diff_k_past is None
            outs = _pre_even(x2, W["norm_mix_g"], W["ev_w_in"], l, e, tm, t, transposed=prompt)
            u, k_rows, v_rows = outs[:3]
            new_k.append(k_rows.reshape(b, t, heads, 2 * DIFF_DH))
            new_v.append(v_rows.reshape(b, t, heads, DIFF_VD))
            u3 = u.reshape(b, t, half)
            new_pool.append(u3[:, t - POOL_HIST:])
            hist16 = jnp.pad(pool_hist[e], ((0, 0), (POOL_HALO - POOL_HIST, 0), (0, 0)))
            m1 = _pool(u3, hist16, W["ev_pool_w"], W["ev_pool_scale"], e, cfg["tp"], pos0,
                       cfg["pool_nseq"]).reshape(n, half)
            lam_init = 0.8 - 0.6 * math.exp(-0.3 * l)
            lam_vecs = [W[nm] for nm in ("ev_lam_q1", "ev_lam_k1", "ev_lam_q2", "ev_lam_k2")]
            if prompt:
                kb, qt, vt = outs[3:]
                m2 = _diff_prompt(qt, kb.reshape(b, t, half), vt, lam_vecs, W["ev_subln_g"], e, lam_init,
                                  cfg["tq"], cfg["hps"])
            else:
                m2 = _diff_sample(outs[3].reshape(b, t, half), diff_k_past, diff_v_past,
                                  k_rows.reshape(b, t * heads, LANES), v_rows.reshape(b, t * heads, LANES),
                                  lam_vecs, W["ev_subln_g"], e, lam_init, cfg["tk"])
            m1, m2 = (m1, 0), (m2.reshape(n, half), 0)
            w_mix, lmix = W["ev_w_out"], e
        else:
            o = l // 2
            rows_first = jnp.broadcast_to(rw_shift[o][:, None, :], (b, min(t, tm_rw), d)).reshape(-1, d)
            outs = _pre_rwkv(x2, rows_first, W["norm_mix_g"], W, l, o, None if o == 0 else v_first, tm_rw, t,
                             cfg["hl_rows"])
            r, k, v, kr, ag, wl, gate, hl = outs
            if o == 0:
                v_first = v
            if cfg["hl_rows"] == tm_rw:
                new_shift.append(hl.reshape(b, t, d)[:, -1])
            else:
                per_seq = t // tm_rw
                new_shift.append(hl.reshape(b, per_seq, cfg["hl_rows"], d)[:, -1, -1])
            arrs = [a.reshape(b, t, d) for a in (r, k, v, kr, ag, wl, gate)]
            y, s_fin = _rwkv_rec(arrs, W["rw_lnx_g"], W["rw_lnx_b"], W["rw_r_k"], rw_state, o, cfg["L"],
                                 cfg["pairs"], cfg["nb"])
            new_s.append(s_fin)
            y2 = y.reshape(n, d)
            m1, m2 = (y2, 0), (y2, 1)
            w_mix, lmix = W["rw_wo"], o
        ffn = (W["norm_ffn_g"], W["ffn_wg"], W["ffn_wu"], W["ffn_wd"], W["final_norm_g"], l == depth - 1)
        if cfg["fuse_ffn"]:
            x2 = _mix_xa(x2, m1, m2, w_mix, lmix, W["norm_xa_g"], W["xa_wq"], mem_k, mem_v, W["xa_wo"], l, tm_xa,
                         t, ffn=ffn)
        else:
            x2 = _mix_xa(x2, m1, m2, w_mix, lmix, W["norm_xa_g"], W["xa_wq"], mem_k, mem_v, W["xa_wo"], l, tm_xa,
                         t)
            x2 = _ffn(x2, *ffn[:5], l, tm, final=ffn[5])
    return (x2.reshape(b, t, d), jnp.stack(new_k), jnp.stack(new_v), jnp.stack(new_pool),
            jnp.stack(new_shift), jnp.stack(new_s))


def kernel(x_prompt, x_sample, cache_diff_k, cache_diff_v, state_pool, state_rw_shift, state_rw_wkv, cache_mem_k, cache_mem_v, mem_prompt, norm_mix_g, norm_xa_g, norm_ffn_g, final_norm_g, ev_w_in, ev_pool_w, ev_pool_scale, ev_lam_q1, ev_lam_k1, ev_lam_q2, ev_lam_k2, ev_subln_g, ev_w_out, rw_mu, rw_wr, rw_wk, rw_wv, rw_wo, rw_w0, rw_w1, rw_w2, rw_a0, rw_a1, rw_a2, rw_v0, rw_v1, rw_v2, rw_g1, rw_g2, rw_k_k, rw_k_a, rw_r_k, rw_lnx_g, rw_lnx_b, xa_wq, xa_wk, xa_wv, xa_wo, ffn_wg, ffn_wu, ffn_wd):
    bf = lambda a: a.astype(BF16)
    vec = lambda a: a.reshape(a.shape[0], 1, -1)
    W = dict(
        norm_mix_g=vec(norm_mix_g), norm_xa_g=vec(norm_xa_g), norm_ffn_g=vec(norm_ffn_g),
        final_norm_g=final_norm_g.reshape(1, -1),
        ev_w_in=bf(ev_w_in), ev_pool_w=bf(ev_pool_w), ev_pool_scale=vec(ev_pool_scale),
        ev_lam_q1=vec(ev_lam_q1), ev_lam_k1=vec(ev_lam_k1), ev_lam_q2=vec(ev_lam_q2), ev_lam_k2=vec(ev_lam_k2),
        ev_subln_g=vec(ev_subln_g), ev_w_out=bf(ev_w_out),
        rw_mu=rw_mu, rw_wr=bf(rw_wr), rw_wk=bf(rw_wk), rw_wv=bf(rw_wv), rw_wo=bf(rw_wo),
        rw_w0=vec(rw_w0), rw_w1=bf(rw_w1), rw_w2=bf(rw_w2), rw_a0=vec(rw_a0), rw_a1=bf(rw_a1), rw_a2=bf(rw_a2),
        rw_v0=vec(rw_v0), rw_v1=bf(rw_v1), rw_v2=bf(rw_v2), rw_g1=bf(rw_g1), rw_g2=bf(rw_g2),
        rw_k_k=vec(rw_k_k), rw_k_a=vec(rw_k_a), rw_r_k=vec(rw_r_k),
        rw_lnx_g=vec(rw_lnx_g), rw_lnx_b=vec(rw_lnx_b),
        xa_wq=bf(xa_wq), xa_wo=bf(xa_wo), ffn_wg=bf(ffn_wg), ffn_wu=bf(ffn_wu), ffn_wd=bf(ffn_wd),
    )
    depth, d = norm_mix_g.shape
    bp, tp_len = x_prompt.shape[:2]
    bs, ts = x_sample.shape[:2]
    n_even = ev_w_in.shape[0]
    n_odd = rw_wr.shape[0]
    nmem = mem_prompt.shape[1]

    kv_f32, kv_bf = _mem_kv(mem_prompt.reshape(bp * nmem, d), bf(jnp.concatenate([xa_wk, xa_wv], axis=0)))
    p_mem_k = kv_f32[:depth].reshape(depth, bp, nmem, XA_HEADS, d // XA_HEADS)
    p_mem_v = kv_f32[depth:].reshape(depth, bp, nmem, XA_HEADS, d // XA_HEADS)
    pmk = kv_bf[:depth].reshape(depth, bp, nmem, d)
    pmv = kv_bf[depth:].reshape(depth, bp, nmem, d)

    cfg_p = dict(tm=512, tm_rw=256, tm_xa=512, tp=512, tq=512, hps=4, tk=None, L=64, pairs=8, nb=2, hl_rows=8,
                 fuse_ffn=True, pool_nseq=1)
    zero_pool = jnp.zeros((n_even, bp, POOL_HIST, d // 2), F32)
    zero_shift = jnp.zeros((n_odd, bp, d), F32)
    zero_wkv = jnp.zeros((n_odd, bp, d // RW_N, RW_N, RW_N), F32)
    outs_p = _trunk(x_prompt, 0, None, None, zero_pool, zero_shift, zero_wkv, pmk, pmv, W, cfg_p)

    past = cache_diff_k.shape[2]
    cfg_s = dict(tm=bs * ts, tm_rw=bs * ts, tm_xa=4 * ts, tp=ts, tq=None, hps=None, tk=2048, L=ts, pairs=8, nb=2,
                 hl_rows=bs * ts, fuse_ffn=False, pool_nseq=8)
    outs_s = _trunk(x_sample, past, cache_diff_k, cache_diff_v, state_pool, state_rw_shift, state_rw_wkv,
                    cache_mem_k, cache_mem_v, W, cfg_s)

    y_p, pk, pv, pp, psh, pS = outs_p
    y_s, sk, sv, sp, ssh, sS = outs_s
    return (y_p, y_s, pk, pv, pp, psh, pS, p_mem_k, p_mem_v, sk, sv, sp, ssh, sS)
```

```python
import functools
import math

import jax
import jax.numpy as jnp
import numpy as np
from jax import lax
from jax.experimental import pallas as pl
from jax.experimental.pallas import tpu as pltpu

F32 = jnp.float32
BF16 = jnp.bfloat16

CHUNK = 64
POOL_GROUPS = 4
POOL_WINDOWS = (2, 4, 8, 16)
POOL_HIST = max(POOL_WINDOWS) - 1
POOL_HALO = 16
DIFF_DH = 64
DIFF_VD = 2 * DIFF_DH
RW_N = 64
RW_LN_EPS = 64e-5
XA_HEADS = 4
NORM_EPS = 1e-6
NEG_INF = -1e30
LOG2E = math.log2(math.e)
LANES = 128

VMEM_LIMIT = 56 * 1024 * 1024


def _cparams(sem):
    return pltpu.CompilerParams(dimension_semantics=sem, vmem_limit_bytes=VMEM_LIMIT)


def _layer_spec(arr, layer, block=None, index=None):
    block = tuple(arr.shape[1:]) if block is None else tuple(block)
    index = (0,) * len(block) if index is None else tuple(index)
    return pl.BlockSpec((None,) + block, lambda *_: (layer,) + index, pipeline_mode=pl.Buffered(1))


def _bdot(a, b):
    return jnp.dot(a.astype(BF16), b.astype(BF16), preferred_element_type=F32)


def _bdot_nt(a, b):
    return lax.dot_general(a.astype(BF16), b.astype(BF16), (((1,), (1,)), ((), ())),
                           preferred_element_type=F32)


def _bdot_tn(a, b):
    return lax.dot_general(a.astype(BF16), b.astype(BF16), (((0,), (0,)), ((), ())),
                           preferred_element_type=F32)


def _rms(x, g):
    return x * lax.rsqrt(jnp.mean(x * x, axis=-1, keepdims=True) + NORM_EPS) * g


def _sigmoid(x):
    return 1.0 / (1.0 + jnp.exp(-x))


def _log2(c):
    assert c > 0 and c & (c - 1) == 0, c
    return c.bit_length() - 1


def _div_pow2(x, c):
    return lax.shift_right_arithmetic(x, _log2(c))


def _mod_pow2(x, c):
    assert c & (c - 1) == 0, c
    return x & (c - 1)


def _memkv_kernel(x_ref, w_ref, o_ref, ob_ref):
    y = _bdot(x_ref[...], w_ref[0])
    o_ref[0] = y
    ob_ref[0] = y.astype(BF16)


def _mem_kv(mem2d, w_stack):
    n, d = mem2d.shape
    nw = w_stack.shape[0]
    return pl.pallas_call(
        _memkv_kernel,
        grid=(nw,),
        in_specs=[pl.BlockSpec((n, d), lambda i: (0, 0)),
                  pl.BlockSpec((1, d, d), lambda i: (i, 0, 0))],
        out_specs=[pl.BlockSpec((1, n, d), lambda i: (i, 0, 0)),
                   pl.BlockSpec((1, n, d), lambda i: (i, 0, 0))],
        out_shape=[jax.ShapeDtypeStruct((nw, n, d), F32),
                   jax.ShapeDtypeStruct((nw, n, d), BF16)],
        compiler_params=_cparams(("parallel",)),
        name="mem_kv",
    )(mem2d, w_stack)


def _pre_even_kernel(x_ref, g_ref, w_ref, u_ref, k_ref, v_ref, *rest, pw, dw, transposed):
    h = _rms(x_ref[...], g_ref[...]).astype(BF16)
    tm = h.shape[0]
    heads = dw // LANES
    proj = lambda lo: jnp.dot(h, w_ref[:, lo:lo + dw], preferred_element_type=F32)
    hsl = [slice(hd * LANES, (hd + 1) * LANES) for hd in range(heads)]
    v = proj(pw + 2 * dw)
    if transposed:
        kb_ref, qt_ref, vt_ref = rest
        for hd in range(heads):
            vt_ref[0, hd] = v[:, hsl[hd]].T.astype(BF16)
    for hd in range(heads):
        v_ref[pl.ds(hd, tm, stride=heads), :] = v[:, hsl[hd]]
    q = proj(pw) * (DIFF_DH ** -0.5)
    if transposed:
        for hd in range(heads):
            qt_ref[0, hd] = (q[:, hsl[hd]] * LOG2E).T.astype(BF16)
    else:
        (q_ref,) = rest
        q_ref[...] = q.astype(BF16)
    k = proj(pw + dw)
    for hd in range(heads):
        k_ref[pl.ds(hd, tm, stride=heads), :] = k[:, hsl[hd]]
    if transposed:
        kb_ref[...] = k.astype(BF16)
    u_ref[...] = proj(0)


def _pre_even(x2d, g, w_in, layer, e, tm, seq_len, transposed):
    n, d = x2d.shape
    pw = d // 2
    dw = d // 2
    heads = dw // LANES
    row = lambda c: pl.BlockSpec((tm, c), lambda i: (i, 0))
    kv_rows = pl.BlockSpec((tm * heads, LANES), lambda i: (i, 0))
    out_specs = [row(pw), kv_rows, kv_rows]
    out_shape = [jax.ShapeDtypeStruct((n, pw), F32), jax.ShapeDtypeStruct((n * heads, LANES), F32),
                 jax.ShapeDtypeStruct((n * heads, LANES), F32)]
    if transposed:
        tps = seq_len // tm
        tspec = pl.BlockSpec((1, heads, LANES, tm), lambda i: (i // tps, 0, 0, i % tps))
        tshape = jax.ShapeDtypeStruct((n // seq_len, heads, LANES, seq_len), BF16)
        out_specs += [row(dw), tspec, tspec]
        out_shape += [jax.ShapeDtypeStruct((n, dw), BF16), tshape, tshape]
    else:
        out_specs += [row(dw)]
        out_shape += [jax.ShapeDtypeStruct((n, dw), BF16)]
    return pl.pallas_call(
        functools.partial(_pre_even_kernel, pw=pw, dw=dw, transposed=transposed),
        grid=(n // tm,),
        in_specs=[row(d), _layer_spec(g, layer), _layer_spec(w_in, e)],
        out_specs=out_specs,
        out_shape=out_shape,
        compiler_params=_cparams(("parallel",)),
        name="pre_even",
    )(x2d, g, w_in)


def _pool_kernel(u_ref, uprev_ref, hist_ref, w_ref, sc_ref, o_ref, *, tp, pos0, gw, nseq):
    t = pl.program_id(1)
    pos = pos0 + t * tp + lax.broadcasted_iota(jnp.int32, (tp, 1), 0)
    pooled = [[] for _ in POOL_WINDOWS]
    for b in range(nseq):
        u = u_ref[b]
        prev = jnp.where(t == 0, hist_ref[b], uprev_ref[b])
        s = jnp.concatenate([prev, u], axis=0)
        for g, (sh, w) in enumerate(zip((1, 2, 4, 8), POOL_WINDOWS)):
            s = s + pltpu.roll(s, sh, 0)
            sl = slice(g * gw, (g + 1) * gw)
            cnt = jnp.minimum(pos + 1, w).astype(F32)
            pooled[g].append(s[POOL_HALO:, sl] / cnt - u[:, sl])
    for g in range(len(POOL_WINDOWS)):
        sl = slice(g * gw, (g + 1) * gw)
        y = _bdot(jnp.concatenate(pooled[g], axis=0), w_ref[g]) * sc_ref[:, sl]
        for b in range(nseq):
            o_ref[b, :, sl] = y[b * tp:(b + 1) * tp].astype(BF16)


def _pool(u3, hist16, pool_w, scale, e, tp, pos0, nseq):
    b, t, pw = u3.shape
    gw = pw // POOL_GROUPS
    hb = tp // POOL_HALO
    return pl.pallas_call(
        functools.partial(_pool_kernel, tp=tp, pos0=pos0, gw=gw, nseq=nseq),
        grid=(b // nseq, t // tp),
        in_specs=[pl.BlockSpec((nseq, tp, pw), lambda i, j: (i, j, 0)),
                  pl.BlockSpec((nseq, POOL_HALO, pw), lambda i, j: (i, jnp.maximum(j * hb - 1, 0), 0)),
                  pl.BlockSpec((nseq, POOL_HALO, pw), lambda i, j: (i, 0, 0)),
                  _layer_spec(pool_w, e), _layer_spec(scale, e)],
        out_specs=pl.BlockSpec((nseq, tp, pw), lambda i, j: (i, j, 0)),
        out_shape=jax.ShapeDtypeStruct((b, t, pw), BF16),
        compiler_params=_cparams(("parallel", "parallel")),
        name="pool_mixer",
    )(u3, u3, hist16, pool_w, scale)


def _lam(lq1, lk1, lq2, lk2, lam_init):
    return (jnp.exp(jnp.sum(lq1[...] * lk1[...], axis=-1, keepdims=True))
            - jnp.exp(jnp.sum(lq2[...] * lk2[...], axis=-1, keepdims=True)) + lam_init)


def _stack_maps(q):
    lane = lax.broadcasted_iota(jnp.int32, q.shape, 1)
    zero = jnp.zeros_like(q)
    return jnp.concatenate([jnp.where(lane < DIFF_DH, q, zero), jnp.where(lane >= DIFF_DH, q, zero)], axis=0)


def _softmax_step(s, v, m_prev, l_prev, acc_prev):
    m_new = jnp.maximum(m_prev, jnp.max(s, axis=-1, keepdims=True))
    alpha = jnp.exp(m_prev - m_new)
    p = jnp.exp(s - m_new)
    l_new = alpha * l_prev + jnp.sum(p, axis=-1, keepdims=True)
    acc_new = alpha * acc_prev + _bdot(p, v)
    return m_new, l_new, acc_new


def _diff_finish(l, acc, lam, subg, rows, out_scale):
    o = acc[:rows] / l[:rows] - lam * (acc[rows:] / l[rows:])
    o = o * lax.rsqrt(jnp.mean(o * o, axis=-1, keepdims=True) + NORM_EPS) * subg
    return o * out_scale


SCORES_AHEAD = 2


def _diffp_kernel(qi_ref, kj_ref, qt_ref, k_ref, vt_ref, bias_ref, lq1, lk1, lq2, lk2, subg_ref, o_ref,
                  q2_ref, m_ref, l_ref, acc_ref, *, tq, hps, lam_init):
    step_id = pl.program_id(2)
    i = qi_ref[step_id]
    j = kj_ref[step_id]

    @pl.when(j == 0)
    def _():
        for hh in range(hps):
            qt = qt_ref[0, hh]
            row = lax.broadcasted_iota(jnp.int32, qt.shape, 0)
            zero = jnp.zeros_like(qt)
            q2_ref[hh, :, :tq] = jnp.where(row < DIFF_DH, qt, zero)
            q2_ref[hh, :, tq:] = jnp.where(row >= DIFF_DH, qt, zero)
        m_ref[...] = jnp.full(m_ref.shape, NEG_INF, F32)
        l_ref[...] = jnp.zeros(l_ref.shape, F32)
        acc_ref[...] = jnp.zeros(acc_ref.shape, F32)

    def step(masked):
        def scores(hh):
            s = jnp.dot(k_ref[0, :, hh * LANES:(hh + 1) * LANES], q2_ref[hh],
                        preferred_element_type=F32)
            return s + bias_ref[...] if masked else s

        def softmax(hh, s):
            m_prev = m_ref[hh]
            m_new = jnp.maximum(m_prev, jnp.max(s, axis=0, keepdims=True))
            alpha = jnp.exp2(m_prev - m_new)
            p = jnp.exp2(s - m_new)
            l_ref[hh] = alpha * l_ref[hh] + jnp.sum(p, axis=0, keepdims=True)
            m_ref[hh] = m_new
            return alpha, p.astype(BF16)

        def values(hh, alpha, p):
            acc_ref[hh] = alpha * acc_ref[hh] + jnp.dot(vt_ref[0, hh], p, preferred_element_type=F32)

        ready = [scores(hh) for hh in range(min(SCORES_AHEAD, hps))]
        pending = None
        for hh in range(hps):
            s_cur = ready.pop(0)
            if hh + SCORES_AHEAD < hps:
                ready.append(scores(hh + SCORES_AHEAD))
            cur = softmax(hh, s_cur)
            if pending is not None:
                values(hh - 1, *pending)
            pending = cur
        values(hps - 1, *pending)

    @pl.when(j < i)
    def _():
        step(False)

    @pl.when(j == i)
    def _():
        step(True)
        lam = _lam(lq1, lk1, lq2, lk2, lam_init)
        for hh in range(hps):
            acc = acc_ref[hh]
            l = l_ref[hh]
            ot = acc[:, :tq] / l[:, :tq] - lam * (acc[:, tq:] / l[:, tq:])
            ot = ot * lax.rsqrt(jnp.mean(ot * ot, axis=0, keepdims=True) + NORM_EPS)
            o_ref[0, :, hh * LANES:(hh + 1) * LANES] = (ot.T * subg_ref[...] * (1.0 - lam_init)).astype(BF16)


def _diff_prompt(qt, kb3, vt, lam_vecs, subg, e, lam_init, tq, hps):
    b, t, w = kb3.shape
    heads = w // LANES
    nq = t // tq
    pairs = [(i, j) for i in range(nq) for j in range(i + 1)]
    qi = jnp.asarray(np.array([p[0] for p in pairs], np.int32))
    kj = jnp.asarray(np.array([p[1] for p in pairs], np.int32))
    key_chunk = np.arange(tq)[:, None] // CHUNK
    query_chunk = (np.arange(2 * tq)[None, :] % tq) // CHUNK
    bias = jnp.asarray(np.where(key_chunk <= query_chunk, 0.0, NEG_INF).astype(np.float32))
    qspec = pl.BlockSpec((1, hps, LANES, tq), lambda bb, h, s, qi_r, kj_r: (bb, h, 0, qi_r[s]))
    kspec = pl.BlockSpec((1, tq, hps * LANES), lambda bb, h, s, qi_r, kj_r: (bb, kj_r[s], h))
    vspec = pl.BlockSpec((1, hps, LANES, tq), lambda bb, h, s, qi_r, kj_r: (bb, h, 0, kj_r[s]))
    vec = lambda a: pl.BlockSpec((None,) + tuple(a.shape[1:]), lambda *_: (e, 0, 0))
    grid_spec = pltpu.PrefetchScalarGridSpec(
        num_scalar_prefetch=2,
        grid=(b, heads // hps, len(pairs)),
        in_specs=[qspec, kspec, vspec,
                  pl.BlockSpec(bias.shape, lambda *_: (0, 0), pipeline_mode=pl.Buffered(1))]
        + [vec(a) for a in lam_vecs] + [vec(subg)],
        out_specs=pl.BlockSpec((1, tq, hps * LANES), lambda bb, h, s, qi_r, kj_r: (bb, qi_r[s], h)),
        scratch_shapes=[pltpu.VMEM((hps, LANES, 2 * tq), BF16), pltpu.VMEM((hps, 1, 2 * tq), F32),
                        pltpu.VMEM((hps, 1, 2 * tq), F32), pltpu.VMEM((hps, LANES, 2 * tq), F32)],
    )
    return pl.pallas_call(
        functools.partial(_diffp_kernel, tq=tq, hps=hps, lam_init=lam_init),
        grid_spec=grid_spec,
        out_shape=jax.ShapeDtypeStruct((b, t, w), BF16),
        compiler_params=_cparams(("parallel", "parallel", "arbitrary")),
        name="diff_attn_prompt",
    )(qi, kj, qt, kb3, vt, bias, *lam_vecs, subg)


def _diffs_kernel(q_ref, ck_ref, cv_ref, kn_ref, vn_ref, lq1, lk1, lq2, lk2, subg_ref, o_ref,
                  m_ref, l_ref, acc_ref, *, ts, heads, lam_init):
    j = pl.program_id(1)
    nj = pl.num_programs(1)
    nq = heads * 2 * ts
    q2 = jnp.concatenate([_stack_maps(q_ref[0, :, h * LANES:(h + 1) * LANES]) for h in range(heads)], axis=0)

    def visible(ncols, valid_cols):
        qh = _div_pow2(lax.broadcasted_iota(jnp.int32, (nq, ncols), 0), 2 * ts)
        col = lax.broadcasted_iota(jnp.int32, (nq, ncols), 1)
        ok = _mod_pow2(col, heads) == qh
        return ok if valid_cols is None else ok & (col < valid_cols)

    @pl.when(j == 0)
    def _():
        nnew = kn_ref.shape[1]
        pad = jnp.zeros((LANES - nnew, LANES), F32)
        kn = jnp.concatenate([kn_ref[0], pad], axis=0)
        vn = jnp.concatenate([vn_ref[0], pad], axis=0)
        s = jnp.where(visible(LANES, nnew), _bdot_nt(q2, kn), NEG_INF)
        m, l, acc = _softmax_step(s, vn, jnp.full((nq, 1), NEG_INF, F32), jnp.zeros((nq, 1), F32),
                                  jnp.zeros((nq, LANES), F32))
        m_ref[...] = m
        l_ref[...] = l
        acc_ref[...] = acc

    s = _bdot_nt(q2, ck_ref[...])
    s = jnp.where(visible(s.shape[1], None), s, NEG_INF)
    m, l, acc = _softmax_step(s, cv_ref[...], m_ref[...], l_ref[...], acc_ref[...])
    m_ref[...] = m
    l_ref[...] = l
    acc_ref[...] = acc

    @pl.when(j == nj - 1)
    def _():
        lam = _lam(lq1, lk1, lq2, lk2, lam_init)
        l_all = l_ref[...]
        acc_all = acc_ref[...]
        for h in range(heads):
            rows = slice(h * 2 * ts, (h + 1) * 2 * ts)
            o_ref[0, :, h * LANES:(h + 1) * LANES] = _diff_finish(
                l_all[rows], acc_all[rows], lam, subg_ref[...], ts, 1.0 - lam_init).astype(BF16)


def _diff_sample(q3, ck5, cv5, kn_rows, vn_rows, lam_vecs, subg, e, lam_init, tk):
    b, ts, w = q3.shape
    nl, _, past, heads, _ = ck5.shape
    assert heads * LANES == w and heads * ts <= LANES
    assert past % CHUNK == 0 and ts <= CHUNK and past % tk == 0
    rows = lambda a: a.reshape(a.shape[:-3] + (a.shape[-3] * heads, LANES))
    qspec = pl.BlockSpec((1, ts, w), lambda bb, j: (bb, 0, 0))
    new = pl.BlockSpec((1, ts * heads, LANES), lambda bb, j: (bb, 0, 0))
    cache = pl.BlockSpec((None, None, tk * heads, LANES), lambda bb, j: (e, bb, j, 0))
    vec = lambda a: pl.BlockSpec((None,) + tuple(a.shape[1:]), lambda *_: (e, 0, 0))
    nq = heads * 2 * ts
    return pl.pallas_call(
        functools.partial(_diffs_kernel, ts=ts, heads=heads, lam_init=lam_init),
        grid=(b, past // tk),
        in_specs=[qspec, cache, cache, new, new] + [vec(a) for a in lam_vecs] + [vec(subg)],
        out_specs=qspec,
        out_shape=jax.ShapeDtypeStruct((b, ts, w), BF16),
        scratch_shapes=[pltpu.VMEM((nq, 1), F32), pltpu.VMEM((nq, 1), F32), pltpu.VMEM((nq, LANES), F32)],
        compiler_params=_cparams(("parallel", "arbitrary")),
        name="diff_attn_sample",
    )(q3, rows(ck5), rows(cv5), kn_rows, vn_rows, *lam_vecs, subg)


def _mixxa_compute(x_ref, m1_ref, m2_ref, w1_ref, w2_ref, g_ref, wq_ref, mk_ref, mv_ref, wo_ref, *, nseq, heads):
    x1 = x_ref[...] + _bdot(m1_ref[...], w1_ref[...]) + _bdot(m2_ref[...], w2_ref[...])
    tm, d = x1.shape
    dh = d // heads
    h = _rms(x1, g_ref[...])
    q = (_bdot(h, wq_ref[...]) * (dh ** -0.5)).astype(BF16)
    rq = tm // nseq
    if len(mk_ref.shape) == 4:
        nrows = mk_ref.shape[1] * heads
        qh = _div_pow2(lax.broadcasted_iota(jnp.int32, (heads * rq, nrows), 0), rq)
        own = _mod_pow2(lax.broadcasted_iota(jnp.int32, (heads * rq, nrows), 1), heads) == qh
        outs = []
        for b in range(nseq):
            qb = q[b * rq:(b + 1) * rq]
            qcat = jnp.concatenate([qb[:, hd * dh:(hd + 1) * dh] for hd in range(heads)], axis=0)
            s = jnp.where(own, _bdot_nt(qcat, mk_ref[b].reshape(nrows, dh)), NEG_INF)
            p = jnp.exp(s - jnp.max(s, axis=-1, keepdims=True))
            p = p / jnp.sum(p, axis=-1, keepdims=True)
            ob = _bdot(p, mv_ref[b].reshape(nrows, dh))
            outs.append(jnp.concatenate([ob[hd * rq:(hd + 1) * rq] for hd in range(heads)], axis=1))
        o = jnp.concatenate(outs, axis=0)
    else:
        q3 = q.reshape(nseq, rq, d)
        hsl = [slice(hd * dh, (hd + 1) * dh) for hd in range(heads)]
        ss = [jnp.einsum("bqd,bkd->bqk", q3[:, :, sl], mk_ref[:, :, sl], preferred_element_type=F32)
              for sl in hsl]
        outs = []
        for hd in range(heads):
            p = jnp.exp(ss[hd] - jnp.max(ss[hd], axis=-1, keepdims=True))
            p = p / jnp.sum(p, axis=-1, keepdims=True)
            outs.append(jnp.einsum("bqk,bkd->bqd", p.astype(BF16), mv_ref[:, :, hsl[hd]],
                                   preferred_element_type=F32))
        o = jnp.concatenate(outs, axis=-1).reshape(tm, d)
    return x1 + _bdot(o, wo_ref[...])


def _mixxa_kernel(*refs, nseq, heads):
    refs[-1][...] = _mixxa_compute(*refs[:-1], nseq=nseq, heads=heads)


def _mixxa_ffn_kernel(*refs, nseq, heads, nchunk, final):
    x2 = _mixxa_compute(*refs[:10], nseq=nseq, heads=heads)
    refs[-1][...] = _ffn_compute(x2, *refs[10:-1], nchunk=nchunk, final=final)


FFN_CHUNKS = 11


def _mix_xa(x2d, m1, m2, w_mix, lmix, g, wq, mk, mv, wo, layer, tm, rows_per_seq, ffn=None):
    n, d = x2d.shape
    half = d // 2
    nseq = max(tm // rows_per_seq, 1)
    tiles_per_seq = max(rows_per_seq // tm, 1)
    row = lambda c: pl.BlockSpec((tm, c), lambda i: (i, 0))
    col = lambda c: pl.BlockSpec((tm, half), lambda i: (i, c))
    tail = (0,) * (mk.ndim - 2)
    mem = pl.BlockSpec((None, nseq) + mk.shape[2:], lambda i: (layer, i // tiles_per_seq) + tail)
    ins = [x2d, m1[0], m2[0], w_mix, w_mix, g, wq, mk, mv, wo]
    specs = [row(d), col(m1[1]), col(m2[1]),
             _layer_spec(w_mix, lmix, (half, d), (0, 0)), _layer_spec(w_mix, lmix, (half, d), (1, 0)),
             _layer_spec(g, layer), _layer_spec(wq, layer), mem, mem, _layer_spec(wo, layer)]
    if ffn is None:
        body = functools.partial(_mixxa_kernel, nseq=nseq, heads=XA_HEADS)
    else:
        gf, wg, wu, wd, gfin, final = ffn
        ins += [gf, wg, wu, wd, gfin]
        specs += [_layer_spec(a, layer) for a in (gf, wg, wu, wd)] + [pl.BlockSpec((1, d), lambda i: (0, 0))]
        body = functools.partial(_mixxa_ffn_kernel, nseq=nseq, heads=XA_HEADS, nchunk=FFN_CHUNKS, final=final)
    return pl.pallas_call(
        body,
        grid=(n // tm,),
        in_specs=specs,
        out_specs=row(d),
        out_shape=jax.ShapeDtypeStruct((n, d), F32),
        compiler_params=_cparams(("parallel",)),
        name="mix_xa" if ffn is None else "mix_xa_ffn",
    )(*ins)


def _ffn_compute(x, g_ref, wg_ref, wu_ref, wd_ref, gf_ref, *, nchunk, final):
    h = _rms(x, g_ref[...]).astype(BF16)
    ff = wg_ref.shape[1]
    fc = ff // nchunk
    acc = x
    for c in range(nchunk):
        sl = slice(c * fc, (c + 1) * fc)
        a = jnp.dot(h, wg_ref[:, sl], preferred_element_type=F32)
        b = jnp.dot(h, wu_ref[:, sl], preferred_element_type=F32)
        acc = acc + _bdot(a * _sigmoid(a) * b, wd_ref[sl, :])
    if final:
        acc = _rms(acc, gf_ref[...])
    return acc


def _ffn_kernel(x_ref, *refs, nchunk, final):
    refs[-1][...] = _ffn_compute(x_ref[...], *refs[:-1], nchunk=nchunk, final=final)


def _ffn(x2d, g, wg, wu, wd, gf, layer, tm, final):
    n, d = x2d.shape
    row = pl.BlockSpec((tm, d), lambda i: (i, 0))
    return pl.pallas_call(
        functools.partial(_ffn_kernel, nchunk=FFN_CHUNKS, final=final),
        grid=(n // tm,),
        in_specs=[row, _layer_spec(g, layer), _layer_spec(wg, layer), _layer_spec(wu, layer),
                  _layer_spec(wd, layer), pl.BlockSpec((1, d), lambda i: (0, 0))],
        out_specs=row,
        out_shape=jax.ShapeDtypeStruct((n, d), F32),
        compiler_params=_cparams(("parallel",)),
        name="ffn",
    )(x2d, g, wg, wu, wd, gf)


_RW_PARAMS = ("mu", "wr", "wk", "wv", "w0", "w1", "w2", "a0", "a1", "a2", "g1", "g2", "k_k", "k_a")


def _pre_rwkv_kernel(*refs, tm, rows_per_seq, hl_rows, has_vres):
    (x_ref, xp_ref, first_ref, g_ref, mu_ref, wr_ref, wk_ref, wv_ref, w0_ref, w1_ref, w2_ref,
     a0_ref, a1_ref, a2_ref, g1_ref, g2_ref, kk_ref, ka_ref) = refs[:18]
    pos = 18
    if has_vres:
        vf_ref, v0_ref, v1_ref, v2_ref = refs[pos:pos + 4]
        pos += 4
    r_ref, k_ref, v_ref, kr_ref, ag_ref, w_ref, gate_ref, hl_ref = refs[pos:]

    i = pl.program_id(0)
    g = g_ref[...]
    h = _rms(x_ref[...], g)
    hp_last = _rms(xp_ref[...], g)[xp_ref.shape[0] - 1:, :]
    rowl = lax.broadcasted_iota(jnp.int32, (tm, 1), 0)
    shifted = jnp.where(rowl == 0, hp_last, pltpu.roll(h, 1, 0))
    is_start = _mod_pow2(i * tm + rowl, rows_per_seq) == 0
    h_prev = jnp.where(is_start, first_ref[...], shifted)
    xx = h_prev - h
    mix = lambda n: (h + xx * mu_ref[n:n + 1, :]).astype(BF16)
    t_w = _bdot(mix(1), w1_ref[...])
    t_a = _bdot(mix(4), a1_ref[...])
    t_g = _bdot(mix(5), g1_ref[...])
    xv = mix(3)
    if has_vres:
        t_v = _bdot(xv, v1_ref[...])
    r_ref[...] = _bdot(mix(0), wr_ref[...]).astype(r_ref.dtype)
    z = w0_ref[...] + _bdot(jnp.tanh(t_w), w2_ref[...])
    w_ref[...] = -(jnp.maximum(-z, 0.0) + jnp.log1p(jnp.exp(-jnp.abs(z)))) - 0.5
    k = _bdot(mix(2), wk_ref[...])
    a = _sigmoid(a0_ref[...] + _bdot(t_a, a2_ref[...]))
    ag_ref[...] = a.astype(ag_ref.dtype)
    v = _bdot(xv, wv_ref[...])
    if has_vres:
        v = v + (vf_ref[...].astype(F32) - v) * _sigmoid(v0_ref[...] + _bdot(t_v, v2_ref[...]))
    v_ref[...] = v.astype(v_ref.dtype)
    gate_ref[...] = _bdot(_sigmoid(t_g), g2_ref[...]).astype(gate_ref.dtype)
    kr_ref[...] = (k * kk_ref[...]).astype(kr_ref.dtype)
    k_ref[...] = (k * (1.0 + (a - 1.0) * ka_ref[...])).astype(k_ref.dtype)
    hl_ref[...] = h[tm - hl_rows:, :]


def _pre_rwkv(x2d, first, g, W, layer, o, vf, tm, rows_per_seq, hl_rows):
    n, d = x2d.shape
    has_vres = vf is not None
    xp_rows = 8
    row = pl.BlockSpec((tm, d), lambda i: (i, 0))
    ins = [x2d, x2d, first, g] + [W["rw_" + nm] for nm in _RW_PARAMS]
    specs = [row,
             pl.BlockSpec((xp_rows, d), lambda i: (jnp.maximum(i * (tm // xp_rows) - 1, 0), 0)),
             pl.BlockSpec((tm, d), lambda i: (i // max(rows_per_seq // tm, 1), 0)),
             _layer_spec(g, layer)] + [_layer_spec(W["rw_" + nm], o) for nm in _RW_PARAMS]
    if has_vres:
        vres = [W["rw_v0"], W["rw_v1"], W["rw_v2"]]
        ins += [vf] + vres
        specs += [row] + [_layer_spec(a, o - 1) for a in vres]
    nt = n // tm
    dts = [BF16, BF16, BF16, BF16, BF16, F32, BF16]
    return pl.pallas_call(
        functools.partial(_pre_rwkv_kernel, tm=tm, rows_per_seq=rows_per_seq, hl_rows=hl_rows,
                          has_vres=has_vres),
        grid=(nt,),
        in_specs=specs,
        out_specs=[row] * 7 + [pl.BlockSpec((hl_rows, d), lambda i: (i, 0))],
        out_shape=[jax.ShapeDtypeStruct((n, d), dt) for dt in dts]
        + [jax.ShapeDtypeStruct((nt * hl_rows, d), F32)],
        compiler_params=_cparams(("parallel",)),
        name="pre_rwkv",
    )(*ins)


def _split3(x):
    x1 = x.astype(BF16)
    r1 = x - x1.astype(F32)
    x2 = r1.astype(BF16)
    x3 = (r1 - x2.astype(F32)).astype(BF16)
    return x1, x2, x3


def _head_sum(x, lo_mask):
    s_lo = jnp.sum(jnp.where(lo_mask, x, 0.0), axis=-1, keepdims=True)
    s_hi = jnp.sum(jnp.where(lo_mask, 0.0, x), axis=-1, keepdims=True)
    return jnp.where(lo_mask, s_lo, s_hi)


def _rwkv_rec_kernel(r_ref, k_ref, v_ref, kr_ref, ag_ref, w_ref, gate_ref, lng_ref, lnb_ref, rk_ref, s0_ref,
                     y_ref, sfin_ref, s_scr, *, L, pairs, nb):
    c = pl.program_id(2)
    nc = pl.num_programs(2)
    P = range(nb * pairs)
    bi = [p // pairs for p in P]
    hp = [p % pairs for p in P]

    @pl.when(c == 0)
    def _():
        zero = jnp.zeros((RW_N, RW_N), F32)
        for p in P:
            top = jnp.concatenate([s0_ref[bi[p], 2 * hp[p]], zero], axis=1)
            bot = jnp.concatenate([zero, s0_ref[bi[p], 2 * hp[p] + 1]], axis=1)
            s_scr[p] = jnp.concatenate([top, bot], axis=0)

    L2 = 2 * L
    sub = min(16, L)
    nblk = L // sub
    lane = lax.broadcasted_iota(jnp.int32, (L, LANES), 1)
    lo = lane < RW_N
    row2 = lax.broadcasted_iota(jnp.int32, (L2, L2), 0)
    col2 = lax.broadcasted_iota(jnp.int32, (L2, L2), 1)
    same = _div_pow2(row2, L) == _div_pow2(col2, L)
    strict = same & (col2 < row2)
    incl = same & (col2 <= row2)
    diag_blk = _div_pow2(row2, sub) == _div_pow2(col2, sub)
    eye = (row2 == col2).astype(F32)
    tri = (lax.broadcasted_iota(jnp.int32, (L, L), 1) <= lax.broadcasted_iota(jnp.int32, (L, L), 0)).astype(BF16)
    dot = lambda a, b: jnp.dot(a, b, preferred_element_type=F32)

    def stack(x):
        xb = x.astype(BF16)
        zero = jnp.zeros_like(xb)
        return jnp.concatenate([jnp.where(lo, xb, zero), jnp.where(lo, zero, xb)], axis=0)

    sl = [slice(hp[p] * LANES, (hp[p] + 1) * LANES) for p in P]
    r = [r_ref[bi[p], :, sl[p]].astype(F32) for p in P]
    k = [k_ref[bi[p], :, sl[p]].astype(F32) for p in P]
    v = [v_ref[bi[p], :, sl[p]].astype(F32) for p in P]
    logd = [-jnp.exp(w_ref[bi[p], :, sl[p]]) for p in P]
    kk = []
    for p in P:
        kr = kr_ref[bi[p], :, sl[p]].astype(F32)
        kk.append(kr / jnp.maximum(jnp.sqrt(_head_sum(kr * kr, lo)), 1e-12))
    cum = []
    for p in P:
        l1, l2, l3 = _split3(logd[p])
        cum.append(dot(tri, l1) + dot(tri, l2) + dot(tri, l3))
    ar_s, bk_s, v_s, bkh_s, p_last = [], [], [], [], []
    for p in P:
        e_pos = jnp.exp(cum[p])
        e_neg = jnp.exp(-cum[p])
        pl_ = e_pos[L - 1:, :]
        bt = kk[p] * ag_ref[bi[p], :, sl[p]].astype(F32) * e_neg
        kt = k[p] * e_neg
        ar_s.append(jnp.concatenate([stack(-kk[p] * jnp.exp(cum[p] - logd[p])), stack(r[p] * e_pos)], axis=0))
        bk_s.append(jnp.concatenate([stack(bt), stack(kt)], axis=0))
        bkh_s.append(jnp.concatenate([stack(bt * pl_), stack(kt * pl_)], axis=0))
        v_s.append(stack(v[p]))
        p_last.append(pl_)
    s_bd = [s_scr[p] for p in P]
    ss = [_bdot_nt(ar_s[p], s_bd[p]) for p in P]
    g_all = [_bdot_nt(ar_s[p], bk_s[p]) for p in P]
    a_ab = [jnp.where(strict, g_all[p][:L2, :L2], 0.0) for p in P]
    a_akrk = [jnp.concatenate([jnp.where(strict, g_all[p][:L2, L2:], 0.0),
                               jnp.where(incl, g_all[p][L2:, L2:], 0.0)], axis=0) for p in P]
    a_rb = [jnp.where(incl, g_all[p][L2:, :L2], 0.0) for p in P]
    sv = [_bdot(a_akrk[p], v_s[p]) for p in P]
    wmat = [ss[p][:L2] + sv[p][:L2] for p in P]

    a_d = [jnp.where(diag_blk, a_ab[p], 0.0) for p in P]
    pm = [eye + a_d[p] for p in P]
    pw = a_d
    for _ in range(_log2(sub) - 1):
        pw = [_bdot(pw[p], pw[p]) for p in P]
        pm = [pm[p] + _bdot(pw[p], pm[p]) for p in P]
    if nblk > 1:
        xn = [_bdot(pm[p], jnp.concatenate([wmat[p], a_ab[p] - a_d[p]], axis=1)) for p in P]
        u = [xn[p][:, :LANES] for p in P]
        nm = [xn[p][:, LANES:] for p in P]
        steps = _log2(nblk)
        for it in range(steps):
            u = [u[p] + _bdot(nm[p], u[p]) for p in P]
            if it < steps - 1:
                nm = [_bdot(nm[p], nm[p]) for p in P]
    else:
        u = [_bdot(pm[p], wmat[p]) for p in P]

    y2 = [ss[p][L2:] + sv[p][L2:] + _bdot(a_rb[p], u[p]) for p in P]
    for p in P:
        uv = jnp.concatenate([u[p].astype(BF16), v_s[p]], axis=0)
        s_scr[p] = s_bd[p] * p_last[p] + _bdot_tn(uv, bkh_s[p])
    for p in P:
        y = y2[p][:L] + y2[p][L:]
        mean = _head_sum(y, lo) * (1.0 / RW_N)
        yc = y - mean
        var = _head_sum(yc * yc, lo) * (1.0 / RW_N)
        yn = yc * lax.rsqrt(var + RW_LN_EPS) * lng_ref[:, sl[p]] + lnb_ref[:, sl[p]]
        bonus = _head_sum(r[p] * k[p] * rk_ref[:, sl[p]], lo) * v[p]
        y_ref[bi[p], :, sl[p]] = ((yn + bonus) * gate_ref[bi[p], :, sl[p]].astype(F32)).astype(BF16)

    @pl.when(c == nc - 1)
    def _():
        for p in P:
            s = s_scr[p]
            sfin_ref[bi[p], 2 * hp[p]] = s[:RW_N, :RW_N]
            sfin_ref[bi[p], 2 * hp[p] + 1] = s[RW_N:, RW_N:]


def _rwkv_rec(arrs, lng, lnb, rk, s0, o, L, pairs, nb):
    b, t, d = arrs[0].shape
    heads = d // RW_N
    groups = heads // (2 * pairs)
    w = pairs * LANES
    tok = pl.BlockSpec((nb, L, w), lambda bb, gg, c: (bb, c, gg))
    vec = pl.BlockSpec((None, 1, w), lambda bb, gg, c: (o, 0, gg))
    st_in = pl.BlockSpec((None, nb, 2 * pairs, RW_N, RW_N), lambda bb, gg, c: (o, bb, gg, 0, 0))
    st_out = pl.BlockSpec((nb, 2 * pairs, RW_N, RW_N), lambda bb, gg, c: (bb, gg, 0, 0))
    return pl.pallas_call(
        functools.partial(_rwkv_rec_kernel, L=L, pairs=pairs, nb=nb),
        grid=(b // nb, groups, t // L),
        in_specs=[tok] * 7 + [vec, vec, vec, st_in],
        out_specs=[tok, st_out],
        out_shape=[jax.ShapeDtypeStruct((b, t, d), BF16),
                   jax.ShapeDtypeStruct((b, heads, RW_N, RW_N), F32)],
        scratch_shapes=[pltpu.VMEM((nb * pairs, LANES, LANES), F32)],
        compiler_params=_cparams(("parallel", "parallel", "arbitrary")),
        name="rwkv_rec",
    )(*arrs, lng, lnb, rk, s0)


def _trunk(x, pos0, diff_k_past, diff_v_past, pool_hist, rw_shift, rw_state, mem_k, mem_v, W, cfg):
    b, t, d = x.shape
    n = b * t
    depth = W["norm_mix_g"].shape[0]
    tm, tm_rw, tm_xa = cfg["tm"], cfg["tm_rw"], cfg["tm_xa"]
    x2 = x.reshape(n, d)
    new_k, new_v, new_pool, new_shift, new_s = [], [], [], [], []
    v_first = None
    half = d // 2
    heads = half // LANES
    for l in range(depth):
        if l % 2 == 0:
            e = l // 2
            prompt = diff_k_past is None
            outs = _pre_even(x2, W["norm_mix_g"], W["ev_w_in"], l, e, tm, t, transposed=prompt)
            u, k_rows, v_rows = outs[:3]
            new_k.append(k_rows.reshape(b, t, heads, 2 * DIFF_DH))
            new_v.append(v_rows.reshape(b, t, heads, DIFF_VD))
            u3 = u.reshape(b, t, half)
            new_pool.append(u3[:, t - POOL_HIST:])
            hist16 = jnp.pad(pool_hist[e], ((0, 0), (POOL_HALO - POOL_HIST, 0), (0, 0)))
            m1 = _pool(u3, hist16, W["ev_pool_w"], W["ev_pool_scale"], e, cfg["tp"], pos0,
                       cfg["pool_nseq"]).reshape(n, half)
            lam_init = 0.8 - 0.6 * math.exp(-0.3 * l)
            lam_vecs = [W[nm] for nm in ("ev_lam_q1", "ev_lam_k1", "ev_lam_q2", "ev_lam_k2")]
            if prompt:
                kb, qt, vt = outs[3:]
                m2 = _diff_prompt(qt, kb.reshape(b, t, half), vt, lam_vecs, W["ev_subln_g"], e, lam_init,
                                  cfg["tq"], cfg["hps"])
            else:
                m2 = _diff_sample(outs[3].reshape(b, t, half), diff_k_past, diff_v_past,
                                  k_rows.reshape(b, t * heads, LANES), v_rows.reshape(b, t * heads, LANES),
                                  lam_vecs, W["ev_subln_g"], e, lam_init, cfg["tk"])
            m1, m2 = (m1, 0), (m2.reshape(n, half), 0)
            w_mix, lmix = W["ev_w_out"], e
        else:
            o = l // 2
            rows_first = jnp.broadcast_to(rw_shift[o][:, None, :], (b, min(t, tm_rw), d)).reshape(-1, d)
            outs = _pre_rwkv(x2, rows_first, W["norm_mix_g"], W, l, o, None if o == 0 else v_first, tm_rw, t,
                             cfg["hl_rows"])
            r, k, v, kr, ag, wl, gate, hl = outs
            if o == 0:
                v_first = v
            if cfg["hl_rows"] == tm_rw:
                new_shift.append(hl.reshape(b, t, d)[:, -1])
            else:
                per_seq = t // tm_rw
                new_shift.append(hl.reshape(b, per_seq, cfg["hl_rows"], d)[:, -1, -1])
            arrs = [a.reshape(b, t, d) for a in (r, k, v, kr, ag, wl, gate)]
            y, s_fin = _rwkv_rec(arrs, W["rw_lnx_g"], W["rw_lnx_b"], W["rw_r_k"], rw_state, o, cfg["L"],
                                 cfg["pairs"], cfg["nb"])
            new_s.append(s_fin)
            y2 = y.reshape(n, d)
            m1, m2 = (y2, 0), (y2, 1)
            w_mix, lmix = W["rw_wo"], o
        ffn = (W["norm_ffn_g"], W["ffn_wg"], W["ffn_wu"], W["ffn_wd"], W["final_norm_g"], l == depth - 1)
        if cfg["fuse_ffn"]:
            x2 = _mix_xa(x2, m1, m2, w_mix, lmix, W["norm_xa_g"], W["xa_wq"], mem_k, mem_v, W["xa_wo"], l, tm_xa,
                         t, ffn=ffn)
        else:
            x2 = _mix_xa(x2, m1, m2, w_mix, lmix, W["norm_xa_g"], W["xa_wq"], mem_k, mem_v, W["xa_wo"], l, tm_xa,
                         t)
            x2 = _ffn(x2, *ffn[:5], l, tm, final=ffn[5])
    return (x2.reshape(b, t, d), jnp.stack(new_k), jnp.stack(new_v), jnp.stack(new_pool),
            jnp.stack(new_shift), jnp.stack(new_s))


def kernel(x_prompt, x_sample, cache_diff_k, cache_diff_v, state_pool, state_rw_shift, state_rw_wkv, cache_mem_k, cache_mem_v, mem_prompt, norm_mix_g, norm_xa_g, norm_ffn_g, final_norm_g, ev_w_in, ev_pool_w, ev_pool_scale, ev_lam_q1, ev_lam_k1, ev_lam_q2, ev_lam_k2, ev_subln_g, ev_w_out, rw_mu, rw_wr, rw_wk, rw_wv, rw_wo, rw_w0, rw_w1, rw_w2, rw_a0, rw_a1, rw_a2, rw_v0, rw_v1, rw_v2, rw_g1, rw_g2, rw_k_k, rw_k_a, rw_r_k, rw_lnx_g, rw_lnx_b, xa_wq, xa_wk, xa_wv, xa_wo, ffn_wg, ffn_wu, ffn_wd):
    bf = lambda a: a.astype(BF16)
    vec = lambda a: a.reshape(a.shape[0], 1, -1)
    W = dict(
        norm_mix_g=vec(norm_mix_g), norm_xa_g=vec(norm_xa_g), norm_ffn_g=vec(norm_ffn_g),
        final_norm_g=final_norm_g.reshape(1, -1),
        ev_w_in=bf(ev_w_in), ev_pool_w=bf(ev_pool_w), ev_pool_scale=vec(ev_pool_scale),
        ev_lam_q1=vec(ev_lam_q1), ev_lam_k1=vec(ev_lam_k1), ev_lam_q2=vec(ev_lam_q2), ev_lam_k2=vec(ev_lam_k2),
        ev_subln_g=vec(ev_subln_g), ev_w_out=bf(ev_w_out),
        rw_mu=rw_mu, rw_wr=bf(rw_wr), rw_wk=bf(rw_wk), rw_wv=bf(rw_wv), rw_wo=bf(rw_wo),
        rw_w0=vec(rw_w0), rw_w1=bf(rw_w1), rw_w2=bf(rw_w2), rw_a0=vec(rw_a0), rw_a1=bf(rw_a1), rw_a2=bf(rw_a2),
        rw_v0=vec(rw_v0), rw_v1=bf(rw_v1), rw_v2=bf(rw_v2), rw_g1=bf(rw_g1), rw_g2=bf(rw_g2),
        rw_k_k=vec(rw_k_k), rw_k_a=vec(rw_k_a), rw_r_k=vec(rw_r_k),
        rw_lnx_g=vec(rw_lnx_g), rw_lnx_b=vec(rw_lnx_b),
        xa_wq=bf(xa_wq), xa_wo=bf(xa_wo), ffn_wg=bf(ffn_wg), ffn_wu=bf(ffn_wu), ffn_wd=bf(ffn_wd),
    )
    depth, d = norm_mix_g.shape
    bp, tp_len = x_prompt.shape[:2]
    bs, ts = x_sample.shape[:2]
    n_even = ev_w_in.shape[0]
    n_odd = rw_wr.shape[0]
    nmem = mem_prompt.shape[1]

    kv_f32, kv_bf = _mem_kv(mem_prompt.reshape(bp * nmem, d), bf(jnp.concatenate([xa_wk, xa_wv], axis=0)))
    p_mem_k = kv_f32[:depth].reshape(depth, bp, nmem, XA_HEADS, d // XA_HEADS)
    p_mem_v = kv_f32[depth:].reshape(depth, bp, nmem, XA_HEADS, d // XA_HEADS)
    pmk = kv_bf[:depth].reshape(depth, bp, nmem, d)
    pmv = kv_bf[depth:].reshape(depth, bp, nmem, d)

    cfg_p = dict(tm=512, tm_rw=256, tm_xa=512, tp=512, tq=512, hps=4, tk=None, L=64, pairs=8, nb=2, hl_rows=8,
                 fuse_ffn=True, pool_nseq=1)
    zero_pool = jnp.zeros((n_even, bp, POOL_HIST, d // 2), F32)
    zero_shift = jnp.zeros((n_odd, bp, d), F32)
    zero_wkv = jnp.zeros((n_odd, bp, d // RW_N, RW_N, RW_N), F32)
    outs_p = _trunk(x_prompt, 0, None, None, zero_pool, zero_shift, zero_wkv, pmk, pmv, W, cfg_p)

    past = cache_diff_k.shape[2]
    cfg_s = dict(tm=bs * ts, tm_rw=bs * ts, tm_xa=4 * ts, tp=ts, tq=None, hps=None, tk=2048, L=ts, pairs=8, nb=2,
                 hl_rows=bs * ts, fuse_ffn=False, pool_nseq=8)
    outs_s = _trunk(x_sample, past, cache_diff_k, cache_diff_v, state_pool, state_rw_shift, state_rw_wkv,
                    cache_mem_k, cache_mem_v, W, cfg_s)

    y_p, pk, pv, pp, psh, pS = outs_p
    y_s, sk, sv, sp, ssh, sS = outs_s
    return (y_p, y_s, pk, pv, pp, psh, pS, p_mem_k, p_mem_v, sk, sv, sp, ssh, sS)
```
